```python
import math
import jax, jax.numpy as jnp
from jax import lax
import numpy as np

D_MODEL = 1024
BATCH = 8
SEQ = 4096
DEPTH = 4

D_MIX = D_MODEL
HEAD_DIM = 64
N_Q_HEADS = 8
N_KV_HEADS = 2
GQA_GROUP = N_Q_HEADS // N_KV_HEADS
D_ATTN = N_Q_HEADS * HEAD_DIM
D_KV = N_KV_HEADS * HEAD_DIM
WINDOW = 128
ATT_BLOCK = 128
N_BUCKETS = 32
MAX_DISTANCE = 128
D_CONV = D_MIX // 4
CONV_WIDTH = 3
D_SGU = D_MIX // 4
SGU_GROUPS = 4
SGU_GROUP_DIM = D_SGU // SGU_GROUPS
SGU_CHUNK = 128
IN_SPLITS = (D_ATTN, D_KV, D_KV, D_CONV, D_CONV, D_CONV, D_SGU, D_SGU)
D_IN = D_ATTN + 2 * D_KV + 3 * D_CONV + 2 * D_SGU
PEER_HEADS = 8
PEER_TOPK = 16
N_KEYS = 128
N_EXPERTS = N_KEYS * N_KEYS
PEER_QDIM = 256
PEER_HALF = PEER_QDIM // 2
PEER_TOKEN_BLOCK = 128
EPS = 1e-6
NEG_INF = -1e30

kernel_name = "hybrid_conv_swa_sgu_peer_adaln"


def rms_norm(x, g):
    xf = x.astype(jnp.float32)
    y = xf * lax.rsqrt(jnp.mean(xf * xf, axis=-1, keepdims=True) + EPS)
    return (y * g.astype(jnp.float32)).astype(x.dtype)


def t5_causal_bucket(dist):
    max_exact = N_BUCKETS // 2
    d = jnp.maximum(dist, 0)
    log_ratio = jnp.log(jnp.maximum(d, 1).astype(jnp.float32) / max_exact) / math.log(MAX_DISTANCE / max_exact)
    large = max_exact + (log_ratio * (N_BUCKETS - max_exact)).astype(jnp.int32)
    large = jnp.minimum(large, N_BUCKETS - 1)
    return jnp.where(d < max_exact, d, large)


def sliding_window_attention(q, k, v, sink, rel_bias):
    B, S = q.shape[0], q.shape[1]
    nb = S // ATT_BLOCK
    qb = q.reshape(B, nb, ATT_BLOCK, N_KV_HEADS, GQA_GROUP, HEAD_DIM)
    kb = k.reshape(B, nb, ATT_BLOCK, N_KV_HEADS, HEAD_DIM)
    vb = v.reshape(B, nb, ATT_BLOCK, N_KV_HEADS, HEAD_DIM)

    def with_prev(t):
        prev = jnp.pad(t[:, :-1], ((0, 0), (1, 0), (0, 0), (0, 0), (0, 0)))
        return jnp.concatenate([prev, t], axis=2)

    kw, vw = with_prev(kb), with_prev(vb)
    logits = jnp.einsum('bnqkgd,bnskd->bnkgqs', qb, kw,
                        preferred_element_type=jnp.float32) * (HEAD_DIM ** -0.5)
    q_idx = jnp.arange(ATT_BLOCK)[:, None]
    s_idx = jnp.arange(2 * ATT_BLOCK)[None, :]
    dist = q_idx + ATT_BLOCK - s_idx
    in_window = (dist >= 0) & (dist < WINDOW)
    blk = jnp.arange(nb)[:, None, None]
    key_exists = (blk * ATT_BLOCK + s_idx[None] - ATT_BLOCK) >= 0
    mask = in_window[None] & key_exists
    bias = rel_bias.astype(jnp.float32)[t5_causal_bucket(dist)]
    bias = bias.transpose(2, 0, 1).reshape(N_KV_HEADS, GQA_GROUP, ATT_BLOCK, 2 * ATT_BLOCK)
    logits = jnp.where(mask[None, :, None, None], logits + bias, NEG_INF)
    sink_l = sink.astype(jnp.float32).reshape(1, 1, N_KV_HEADS, GQA_GROUP, 1, 1)
    m = jnp.maximum(jnp.max(logits, axis=-1, keepdims=True), sink_l)
    p = jnp.exp(logits - m)
    p = p / (jnp.sum(p, axis=-1, keepdims=True) + jnp.exp(sink_l - m))
    out = jnp.einsum('bnkgqs,bnskd->bnqkgd', p.astype(v.dtype), vw)
    return out.reshape(B, S, D_ATTN)


def short_conv_mixer(b_gate, c_gate, h, conv_w):
    S = h.shape[1]
    z = c_gate * h
    zp = jnp.pad(z, ((0, 0), (CONV_WIDTH - 1, 0), (0, 0)))
    conv = zp[:, 0:S] * conv_w[0]
    for j in range(1, CONV_WIDTH):
        conv = conv + zp[:, j:j + S] * conv_w[j]
    return b_gate * conv


def chunked_spatial_gating(u, v, w_s, b_s):
    B, S = u.shape[0], u.shape[1]
    nc = S // SGU_CHUNK
    vf = v.astype(jnp.float32).reshape(B, S, SGU_GROUPS, SGU_GROUP_DIM)
    mu = jnp.mean(vf, axis=-1, keepdims=True)
    var = jnp.mean(jnp.square(vf - mu), axis=-1, keepdims=True)
    vn = ((vf - mu) * lax.rsqrt(var + EPS)).astype(v.dtype)
    vn = vn.reshape(B, nc, SGU_CHUNK, SGU_GROUPS, SGU_GROUP_DIM)
    causal = jnp.tril(jnp.ones((SGU_CHUNK, SGU_CHUNK), dtype=w_s.dtype))
    w = w_s * causal[None]
    mixed = jnp.einsum('gts,bnsgc->bntgc', w, vn) + b_s.T[None, None, :, :, None]
    return u * mixed.reshape(B, S, D_SGU)


def peer_ffn(h, w_pq, sub_keys, expert_down, expert_up):
    B, S, D = h.shape
    tokens = h.reshape(-1, PEER_TOKEN_BLOCK, D)

    def block(t):
        T = t.shape[0]
        q = (t @ w_pq).reshape(T, PEER_HEADS, 2, PEER_HALF)
        s = jnp.einsum('thpk,hpnk->thpn', q, sub_keys, preferred_element_type=jnp.float32)
        top_s, top_i = lax.top_k(s, PEER_TOPK)
        cand_s = top_s[:, :, 0, :, None] + top_s[:, :, 1, None, :]
        cand_i = top_i[:, :, 0, :, None] * N_KEYS + top_i[:, :, 1, None, :]
        best_s, best_pos = lax.top_k(cand_s.reshape(T, PEER_HEADS, PEER_TOPK * PEER_TOPK), PEER_TOPK)
        experts = jnp.take_along_axis(cand_i.reshape(T, PEER_HEADS, PEER_TOPK * PEER_TOPK), best_pos, axis=-1)
        gate = jax.nn.softmax(best_s, axis=-1)
        u = expert_down[experts]
        act = jax.nn.gelu(jnp.einsum('td,thkd->thk', t, u, preferred_element_type=jnp.float32),
                          approximate=False)
        wgt = (gate * act).astype(t.dtype)
        vv = expert_up[experts]
        return jnp.einsum('thk,thkd->td', wgt, vv)

    return lax.map(block, tokens).reshape(B, S, D)


def hybrid_layer(x, c_act, rel_bias, w_ada, b_ada, norm1_g, norm2_g, w_in, q_norm_g, k_norm_g,
                 attn_sink, conv_w, sgu_w, sgu_b, out_norm_g, w_out, peer_wq, peer_sub_keys,
                 peer_down, peer_up):
    B, S = x.shape[0], x.shape[1]
    mod = (c_act @ w_ada + b_ada)[:, None, :]
    sh1, sc1, g1, sh2, sc2, g2 = jnp.split(mod, 6, axis=-1)
    h = rms_norm(x, norm1_g) * (1 + sc1) + sh1
    proj = h @ w_in
    split_pts = [int(i) for i in np.cumsum(IN_SPLITS)[:-1]]
    q, k, v, cb, cc, ch, su, sv = jnp.split(proj, split_pts, axis=-1)
    q = rms_norm(q.reshape(B, S, N_Q_HEADS, HEAD_DIM), q_norm_g)
    k = rms_norm(k.reshape(B, S, N_KV_HEADS, HEAD_DIM), k_norm_g)
    v = v.reshape(B, S, N_KV_HEADS, HEAD_DIM)
    y_attn = sliding_window_attention(q, k, v, attn_sink, rel_bias)
    y_conv = short_conv_mixer(cb, cc, ch, conv_w)
    y_sgu = chunked_spatial_gating(su, sv, sgu_w, sgu_b)
    ga, gc, gs = jnp.split(out_norm_g, [D_ATTN, D_ATTN + D_CONV])
    merged = jnp.concatenate([rms_norm(y_attn, ga), rms_norm(y_conv, gc), rms_norm(y_sgu, gs)], axis=-1)
    x = x + g1 * (merged @ w_out)
    h2 = rms_norm(x, norm2_g) * (1 + sc2) + sh2
    x = x + g2 * peer_ffn(h2, peer_wq, peer_sub_keys, peer_down, peer_up)
    return x


def setup_inputs(seed: int = 0) -> dict:
    key = jax.random.key(seed)
    ks = jax.random.split(key, 20)
    f32 = jnp.float32
    nrm = lambda k, shape, s: jax.random.normal(k, shape, f32) * s
    return {
        "x": nrm(ks[0], (BATCH, SEQ, D_MODEL), 1.0),
        "c": nrm(ks[1], (BATCH, D_MODEL), 1.0),
        "rel_bias": nrm(ks[2], (N_BUCKETS, N_Q_HEADS), 0.5),
        "w_ada": nrm(ks[3], (DEPTH, D_MODEL, 6 * D_MODEL), 0.5 * D_MODEL ** -0.5),
        "b_ada": nrm(ks[4], (DEPTH, 6 * D_MODEL), 0.02),
        "norm1_g": 1.0 + nrm(ks[5], (DEPTH, D_MODEL), 0.02),
        "norm2_g": 1.0 + nrm(ks[6], (DEPTH, D_MODEL), 0.02),
        "w_in": nrm(ks[7], (DEPTH, D_MODEL, D_IN), D_MODEL ** -0.5),
        "q_norm_g": 1.0 + nrm(ks[8], (DEPTH, HEAD_DIM), 0.02),
        "k_norm_g": 1.0 + nrm(ks[9], (DEPTH, HEAD_DIM), 0.02),
        "attn_sink": nrm(ks[10], (DEPTH, N_Q_HEADS), 0.5),
        "conv_w": nrm(ks[11], (DEPTH, CONV_WIDTH, D_CONV), CONV_WIDTH ** -0.5),
        "sgu_w": nrm(ks[12], (DEPTH, SGU_GROUPS, SGU_CHUNK, SGU_CHUNK), SGU_CHUNK ** -0.5),
        "sgu_b": 1.0 + nrm(ks[13], (DEPTH, SGU_GROUPS, SGU_CHUNK), 0.02),
        "out_norm_g": 1.0 + nrm(ks[14], (DEPTH, D_MIX), 0.02),
        "w_out": nrm(ks[15], (DEPTH, D_MIX, D_MODEL), D_MIX ** -0.5),
        "peer_wq": nrm(ks[16], (DEPTH, D_MODEL, PEER_HEADS * PEER_QDIM), D_MODEL ** -0.5),
        "peer_sub_keys": nrm(ks[17], (DEPTH, PEER_HEADS, 2, N_KEYS, PEER_HALF), PEER_HALF ** -0.5),
        "peer_down": nrm(ks[18], (DEPTH, N_EXPERTS, D_MODEL), D_MODEL ** -0.5),
        "peer_up": nrm(ks[19], (DEPTH, N_EXPERTS, D_MODEL), 0.5 * PEER_HEADS ** -0.5),
    }


def reference(x, c, rel_bias, w_ada, b_ada, norm1_g, norm2_g, w_in, q_norm_g, k_norm_g, attn_sink,
              conv_w, sgu_w, sgu_b, out_norm_g, w_out, peer_wq, peer_sub_keys, peer_down, peer_up):
    c_act = jax.nn.silu(c)
    for l in range(DEPTH):
        x = hybrid_layer(x, c_act, rel_bias, w_ada[l], b_ada[l], norm1_g[l], norm2_g[l], w_in[l],
                         q_norm_g[l], k_norm_g[l], attn_sink[l], conv_w[l], sgu_w[l], sgu_b[l],
                         out_norm_g[l], w_out[l], peer_wq[l], peer_sub_keys[l], peer_down[l], peer_up[l])
    return x
```

```python
import functools
import math

import jax
import jax.numpy as jnp
from jax import lax
from jax.experimental import pallas as pl
from jax.experimental.pallas import tpu as pltpu

D_MODEL = 1024
HEAD_DIM = 64
N_Q_HEADS = 8
N_KV_HEADS = 2
GQA_GROUP = N_Q_HEADS // N_KV_HEADS
D_ATTN = N_Q_HEADS * HEAD_DIM
D_KV = N_KV_HEADS * HEAD_DIM
WINDOW = 128
ATT_BLOCK = 128
N_BUCKETS = 32
MAX_DISTANCE = 128
D_CONV = 256
CONV_WIDTH = 3
D_SGU = 256
SGU_GROUPS = 4
SGU_GROUP_DIM = 64
SGU_CHUNK = 128
D_IN = 2048
PEER_HEADS = 8
PEER_TOPK = 16
N_KEYS = 128
N_EXPERTS = N_KEYS * N_KEYS
PEER_HALF = 128
EPS = 1e-6
NEG_INF = -1e30
SQRT_HALF = 0.7071067811865476

V7X_VMEM_LIMIT_BYTES = 56 * 1024 * 1024
LANES = 128
SUBLANES = 8

N_EXTRACT = PEER_TOPK + 1
T_ROWS = 24
CAND_ROWS = T_ROWS + 7 * SUBLANES + (T_ROWS - SUBLANES)

f32 = jnp.float32
bf16 = jnp.bfloat16


def _dot(a, b):
    return jnp.dot(a, b, preferred_element_type=f32)


def _group_sum(v, bd):
    hi = v.astype(bf16)
    lo = (v - hi.astype(f32)).astype(bf16)
    return _dot(hi, bd) + _dot(lo, bd)


def _adaln_body(c_ref, w_ref, b_ref, o_ref):
    c = c_ref[...]
    c_act = c * (1.0 / (1.0 + jnp.exp(-c)))
    o_ref[...] = _dot(c_act.astype(bf16), w_ref[...].astype(bf16)) + b_ref[...]


def _adaln(c, w_ada, b_ada):
    depth, d, _ = w_ada.shape
    batch = c.shape[0]
    out = pl.pallas_call(
        _adaln_body,
        grid=(depth, 6),
        in_specs=[
            pl.BlockSpec((batch, d), lambda l, j: (0, 0)),
            pl.BlockSpec((None, d, d), lambda l, j: (l, 0, j)),
            pl.BlockSpec((None, None, 1, d), lambda l, j: (l, j, 0, 0)),
        ],
        out_specs=pl.BlockSpec((None, None, batch, d), lambda l, j: (l, j, 0, 0)),
        out_shape=jax.ShapeDtypeStruct((depth, 6, batch, d), f32),
        name="adaln",
    )(c, w_ada, b_ada.reshape(depth, 6, 1, d))
    return out.transpose(0, 2, 1, 3)


def _mixer_body(x_ref, mod_ref, n1g_ref, win_ref, qg_ref, kg_ref, sink_ref, bias_ref, convw_ref,
                sguw_ref, sgub_ref, ong_ref, wout_ref, bd_ref, o_ref, kv_carry, z_carry, *, ts):
    s_idx = pl.program_id(1)

    @pl.when(s_idx == 0)
    def _():
        kv_carry[...] = jnp.zeros_like(kv_carry)
        z_carry[...] = jnp.zeros_like(z_carry)

    x = x_ref[...]
    mod = mod_ref[...]
    sh1, sc1, g1 = mod[0:1], mod[1:2], mod[2:3]
    h = x * lax.rsqrt(jnp.mean(x * x, axis=-1, keepdims=True) + EPS) * n1g_ref[...]
    h = h * (1.0 + sc1) + sh1
    proj = _dot(h.astype(bf16), win_ref[...])

    bd = bd_ref[...]
    q = proj[:, 0:D_ATTN]
    k = proj[:, D_ATTN:D_ATTN + D_KV]
    v = proj[:, D_ATTN + D_KV:D_ATTN + 2 * D_KV]
    o0 = D_ATTN + 2 * D_KV
    cb = proj[:, o0:o0 + D_CONV]
    cc = proj[:, o0 + D_CONV:o0 + 2 * D_CONV]
    ch = proj[:, o0 + 2 * D_CONV:o0 + 3 * D_CONV]
    o1 = o0 + 3 * D_CONV
    su = proj[:, o1:o1 + D_SGU]
    sv = proj[:, o1 + D_SGU:o1 + 2 * D_SGU]

    inv_hd = 1.0 / HEAD_DIM
    qn = q * lax.rsqrt(_group_sum(q * q, bd) * inv_hd + EPS) * qg_ref[...]
    qn = (qn * (HEAD_DIM ** -0.5)).astype(bf16)
    kn = k * lax.rsqrt(_group_sum(k * k, bd[0:D_KV, 0:D_KV]) * inv_hd + EPS) * kg_ref[...]
    k_ext = jnp.concatenate([kv_carry[:, 0:D_KV], kn], axis=0).astype(bf16)
    v_ext = jnp.concatenate([kv_carry[:, D_KV:2 * D_KV], v], axis=0).astype(bf16)

    col = lax.broadcasted_iota(jnp.int32, (1, 2 * ATT_BLOCK), 1)
    no_prev = jnp.where(col < ATT_BLOCK, jnp.where(s_idx == 0, NEG_INF, 0.0), 0.0)

    attn_blocks = []
    for n in range(ts // ATT_BLOCK):
        q_blk = qn[n * ATT_BLOCK:(n + 1) * ATT_BLOCK]
        k_blk = k_ext[n * ATT_BLOCK:(n + 2) * ATT_BLOCK]
        v_blk = v_ext[n * ATT_BLOCK:(n + 2) * ATT_BLOCK]
        heads = []
        for hq in range(N_Q_HEADS):
            g = hq // GQA_GROUP
            qh = q_blk[:, hq * HEAD_DIM:(hq + 1) * HEAD_DIM]
            kh = k_blk[:, g * HEAD_DIM:(g + 1) * HEAD_DIM]
            vh = v_blk[:, g * HEAD_DIM:(g + 1) * HEAD_DIM]
            logits = lax.dot_general(qh, kh, (((1,), (1,)), ((), ())),
                                     preferred_element_type=f32) + bias_ref[hq]
            if n == 0:
                logits = logits + no_prev
            sink = sink_ref[hq]
            m = jnp.maximum(jnp.max(logits, axis=-1, keepdims=True), sink)
            p = jnp.exp(logits - m)
            denom = jnp.sum(p, axis=-1, keepdims=True) + jnp.exp(sink - m)
            heads.append(_dot(p.astype(bf16), vh) * (1.0 / denom))
        attn_blocks.append(jnp.concatenate(heads, axis=-1))
    y_attn = jnp.concatenate(attn_blocks, axis=0)

    z = cc * ch
    row = lax.broadcasted_iota(jnp.int32, (ts, D_CONV), 0)
    zc = z_carry[...]
    z1 = jnp.where(row == 0, zc[7:8], pltpu.roll(z, 1, 0))
    z2 = jnp.where(row == 0, zc[6:7], jnp.where(row == 1, zc[7:8], pltpu.roll(z, 2, 0)))
    cw = convw_ref[...]
    y_conv = cb * (z2 * cw[0:1] + z1 * cw[1:2] + z * cw[2:3])

    bd_s = bd[0:D_SGU, 0:D_SGU]
    inv_g = 1.0 / SGU_GROUP_DIM
    mu = _group_sum(sv, bd_s) * inv_g
    dv = sv - mu
    var = _group_sum(dv * dv, bd_s) * inv_g
    vn = (dv * lax.rsqrt(var + EPS)).astype(bf16)
    r_i = lax.broadcasted_iota(jnp.int32, (SGU_CHUNK, SGU_CHUNK), 0)
    c_i = lax.broadcasted_iota(jnp.int32, (SGU_CHUNK, SGU_CHUNK), 1)
    lane_grp = lax.broadcasted_iota(jnp.int32, (1, D_SGU), 1) // SGU_GROUP_DIM
    w_tril = [jnp.where(r_i >= c_i, sguw_ref[g], 0.0).astype(bf16) for g in range(SGU_GROUPS)]
    sgub = sgub_ref[...]
    mixed_chunks = []
    for n in range(ts // SGU_CHUNK):
        vn_c = vn[n * SGU_CHUNK:(n + 1) * SGU_CHUNK]
        mixed = sgub
        for g in range(SGU_GROUPS):
            mixed = mixed + jnp.where(lane_grp == g, _dot(w_tril[g], vn_c), 0.0)
        mixed_chunks.append(mixed)
    y_sgu = su * jnp.concatenate(mixed_chunks, axis=0)

    ong = ong_ref[...]

    def rms(y, gain):
        return y * lax.rsqrt(jnp.mean(y * y, axis=-1, keepdims=True) + EPS) * gain

    merged = jnp.concatenate([
        rms(y_attn, ong[:, 0:D_ATTN]),
        rms(y_conv, ong[:, D_ATTN:D_ATTN + D_CONV]),
        rms(y_sgu, ong[:, D_ATTN + D_CONV:]),
    ], axis=-1)
    o_ref[...] = x + g1 * _dot(merged.astype(bf16), wout_ref[...])

    kv_carry[:, 0:D_KV] = kn[ts - ATT_BLOCK:ts]
    kv_carry[:, D_KV:2 * D_KV] = v[ts - ATT_BLOCK:ts]
    z_carry[...] = z[ts - SUBLANES:ts]


def _mixer(x, mod, n1g, w_in, qg, kg, sink, bias_tbl, conv_w, sgu_w, sgu_bfull, ong, w_out, bd, *, ts):
    batch, seq, d = x.shape
    full = lambda shape: pl.BlockSpec(shape, lambda b, s: (0,) * len(shape))
    return pl.pallas_call(
        functools.partial(_mixer_body, ts=ts),
        grid=(batch, seq // ts),
        in_specs=[
            pl.BlockSpec((None, ts, d), lambda b, s: (b, s, 0)),
            pl.BlockSpec((None, 6, d), lambda b, s: (b, 0, 0)),
            full((1, d)),
            full((d, D_IN)),
            full((1, D_ATTN)),
            full((1, D_KV)),
            pl.BlockSpec(memory_space=pltpu.SMEM),
            full((N_Q_HEADS, ATT_BLOCK, 2 * ATT_BLOCK)),
            full((CONV_WIDTH, D_CONV)),
            full((SGU_GROUPS, SGU_CHUNK, SGU_CHUNK)),
            full((SGU_CHUNK, D_SGU)),
            full((1, d)),
            full((d, d)),
            full((D_ATTN, D_ATTN)),
        ],
        out_specs=pl.BlockSpec((None, ts, d), lambda b, s: (b, s, 0)),
        out_shape=jax.ShapeDtypeStruct(x.shape, f32),
        scratch_shapes=[
            pltpu.VMEM((ATT_BLOCK, 2 * D_KV), f32),
            pltpu.VMEM((SUBLANES, D_CONV), f32),
        ],
        compiler_params=pltpu.CompilerParams(
            dimension_semantics=("parallel", "arbitrary"),
            vmem_limit_bytes=V7X_VMEM_LIMIT_BYTES),
        name="mixer",
    )(x, mod, n1g, w_in, qg, kg, sink, bias_tbl, conv_w, sgu_w, sgu_bfull, ong, w_out, bd)


def _extract_top(s, t_ref, slot):
    t_ref[slot] = jnp.full(t_ref.shape[1:], NEG_INF, f32)
    for a in range(N_EXTRACT):
        m = jnp.max(s, axis=0, keepdims=True)
        t_ref[slot, a:a + 1, :] = m
        s = jnp.where(s == m, NEG_INF, s)


def _peer_body(x_ref, mod_ref, n2g_ref, wpq_ref, keys_ref, down_ref, up_ref, o_ref,
               h2t_ref, s2_ref, thr_ref, e1_ref, e2_ref, t_ref, w_ref, acc_ref, *, ts, eb):
    e_idx = pl.program_id(1)
    ni = eb // N_KEYS

    @pl.when(e_idx == 0)
    def _():
        x = x_ref[...]
        mod = mod_ref[...]
        sh2, sc2 = mod[3:4], mod[4:5]
        h2 = x * lax.rsqrt(jnp.mean(x * x, axis=-1, keepdims=True) + EPS) * n2g_ref[...]
        h2 = h2 * (1.0 + sc2) + sh2
        h2t = h2.T.astype(bf16)
        h2t_ref[...] = h2t
        qt = _dot(wpq_ref[...], h2t).astype(bf16)
        for hh in range(PEER_HEADS):
            s_p = []
            for p in range(2):
                r = (hh * 2 + p) * PEER_HALF
                s = _dot(keys_ref[hh * 2 + p], qt[r:r + PEER_HALF])
                _extract_top(s, t_ref, p)
                s_p.append(s)
            t1 = t_ref[0]
            t2 = t_ref[1]
            m1 = t1[0:1]
            m2 = t2[0:1]
            cand = [t1[0:1] + t2]
            for a in range(1, SUBLANES):
                cand.append(t1[a:a + 1] + t2[0:SUBLANES])
            cand.append(t1[SUBLANES:T_ROWS] + t2[0:1])
            cand = jnp.concatenate(cand, axis=0)
            c = cand
            for a in range(N_EXTRACT):
                m = jnp.max(c, axis=0, keepdims=True)
                if a == PEER_TOPK - 1:
                    c16 = m
                c = jnp.where(c == m, NEG_INF, c)
            tau = 0.5 * (c16 + m)
            z = jnp.sum(jnp.where(cand >= tau, jnp.exp(cand - (m1 + m2)), 0.0), axis=0, keepdims=True)
            s2_ref[hh] = s_p[1]
            thr_ref[hh] = tau - s_p[0]
            e1_ref[hh] = jnp.exp(s_p[0] - m1) * (1.0 / z)
            e2_ref[hh] = jnp.exp(s_p[1] - m2)
        acc_ref[...] = jnp.zeros_like(acc_ref)

    a_t = _dot(down_ref[...], h2t_ref[...])
    for ii in range(ni):
        i = e_idx * ni + ii
        g = jnp.zeros((N_KEYS, ts), f32)
        for hh in range(PEER_HEADS):
            thr_row = thr_ref[hh, pl.ds(i, 1), :]
            e1_row = e1_ref[hh, pl.ds(i, 1), :]
            g = g + jnp.where(s2_ref[hh] >= thr_row, e1_row * e2_ref[hh], 0.0)
        a_i = a_t[ii * N_KEYS:(ii + 1) * N_KEYS]
        act = 0.5 * a_i * (1.0 + lax.erf(a_i * SQRT_HALF))
        w_ref[ii * N_KEYS:(ii + 1) * N_KEYS, :] = (g * act).astype(bf16)
    acc_ref[...] += _dot(up_ref[...], w_ref[...])

    @pl.when(e_idx == pl.num_programs(1) - 1)
    def _():
        g2 = mod_ref[...][5:6]
        o_ref[...] = x_ref[...] + g2 * acc_ref[...].T


def _peer(x, mod, n2g, wpq_t, keys, down, up_t, *, ts, eb):
    batch, seq, d = x.shape
    n_tok = batch * seq
    tiles_per_seq = seq // ts
    x2 = x.reshape(n_tok, d)
    out = pl.pallas_call(
        functools.partial(_peer_body, ts=ts, eb=eb),
        grid=(n_tok // ts, N_EXPERTS // eb),
        in_specs=[
            pl.BlockSpec((ts, d), lambda t, e: (t, 0)),
            pl.BlockSpec((None, 6, d), lambda t, e: (t // tiles_per_seq, 0, 0)),
            pl.BlockSpec((1, d), lambda t, e: (0, 0)),
            pl.BlockSpec(wpq_t.shape, lambda t, e: (0, 0)),
            pl.BlockSpec(keys.shape, lambda t, e: (0, 0, 0)),
            pl.BlockSpec((eb, d), lambda t, e: (e, 0)),
            pl.BlockSpec((d, eb), lambda t, e: (0, e)),
        ],
        out_specs=pl.BlockSpec((ts, d), lambda t, e: (t, 0)),
        out_shape=jax.ShapeDtypeStruct((n_tok, d), f32),
        scratch_shapes=[
            pltpu.VMEM((d, ts), bf16),
            pltpu.VMEM((PEER_HEADS, N_KEYS, ts), f32),
            pltpu.VMEM((PEER_HEADS, N_KEYS, ts), f32),
            pltpu.VMEM((PEER_HEADS, N_KEYS, ts), f32),
            pltpu.VMEM((PEER_HEADS, N_KEYS, ts), f32),
            pltpu.VMEM((2, T_ROWS, ts), f32),
            pltpu.VMEM((eb, ts), bf16),
            pltpu.VMEM((d, ts), f32),
        ],
        compiler_params=pltpu.CompilerParams(
            dimension_semantics=("parallel", "arbitrary"),
            vmem_limit_bytes=V7X_VMEM_LIMIT_BYTES),
        name="peer",
    )(x2, mod, n2g, wpq_t, keys, down, up_t)
    return out.reshape(batch, seq, d)


def _t5_causal_bucket(dist):
    max_exact = N_BUCKETS // 2
    d = jnp.maximum(dist, 0)
    log_ratio = jnp.log(jnp.maximum(d, 1).astype(f32) / max_exact) / math.log(MAX_DISTANCE / max_exact)
    large = max_exact + (log_ratio * (N_BUCKETS - max_exact)).astype(jnp.int32)
    large = jnp.minimum(large, N_BUCKETS - 1)
    return jnp.where(d < max_exact, d, large)


def _bias_table(rel_bias):
    q_idx = jnp.arange(ATT_BLOCK)[:, None]
    s_idx = jnp.arange(2 * ATT_BLOCK)[None, :]
    dist = q_idx + ATT_BLOCK - s_idx
    in_window = (dist >= 0) & (dist < WINDOW)
    bias = rel_bias.astype(f32)[_t5_causal_bucket(dist)].transpose(2, 0, 1)
    return jnp.where(in_window[None], bias, NEG_INF)


def _forward(x, c, rel_bias, w_ada, b_ada, norm1_g, norm2_g, w_in, q_norm_g, k_norm_g, attn_sink,
             conv_w, sgu_w, sgu_b, out_norm_g, w_out, peer_wq, peer_sub_keys, peer_down, peer_up,
             *, mixer_ts, peer_ts, peer_eb):
    depth = w_ada.shape[0]
    d = x.shape[-1]
    mod = _adaln(c, w_ada, b_ada)
    bias_tbl = _bias_table(rel_bias)
    grp = jnp.arange(D_ATTN) // HEAD_DIM
    bd = (grp[:, None] == grp[None, :]).astype(bf16)
    for l in range(depth):
        sgu_bfull = jnp.repeat(sgu_b[l].T, SGU_GROUP_DIM, axis=1)
        x = _mixer(
            x, mod[l], norm1_g[l].reshape(1, d), w_in[l].astype(bf16),
            jnp.tile(q_norm_g[l], N_Q_HEADS).reshape(1, D_ATTN),
            jnp.tile(k_norm_g[l], N_KV_HEADS).reshape(1, D_KV),
            attn_sink[l], bias_tbl, conv_w[l], sgu_w[l], sgu_bfull,
            out_norm_g[l].reshape(1, d), w_out[l].astype(bf16), bd, ts=mixer_ts)
        x = _peer(
            x, mod[l], norm2_g[l].reshape(1, d), peer_wq[l].T.astype(bf16),
            peer_sub_keys[l].reshape(2 * PEER_HEADS, N_KEYS, PEER_HALF).astype(bf16),
            peer_down[l].astype(bf16), peer_up[l].T.astype(bf16), ts=peer_ts, eb=peer_eb)
    return x


def kernel(x, c, rel_bias, w_ada, b_ada, norm1_g, norm2_g, w_in, q_norm_g, k_norm_g, attn_sink, conv_w,
           sgu_w, sgu_b, out_norm_g, w_out, peer_wq, peer_sub_keys, peer_down, peer_up):
    return _forward(x, c, rel_bias, w_ada, b_ada, norm1_g, norm2_g, w_in, q_norm_g, k_norm_g, attn_sink,
                    conv_w, sgu_w, sgu_b, out_norm_g, w_out, peer_wq, peer_sub_keys, peer_down, peer_up,
                    mixer_ts=512, peer_ts=512, peer_eb=1024)
```

```python
import functools
import math

import jax
import jax.numpy as jnp
from jax import lax
from jax.experimental import pallas as pl
from jax.experimental.pallas import tpu as pltpu

D_MODEL = 1024
HEAD_DIM = 64
N_Q_HEADS = 8
N_KV_HEADS = 2
GQA_GROUP = N_Q_HEADS // N_KV_HEADS
D_ATTN = N_Q_HEADS * HEAD_DIM
D_KV = N_KV_HEADS * HEAD_DIM
WINDOW = 128
ATT_BLOCK = 128
N_BUCKETS = 32
MAX_DISTANCE = 128
D_CONV = 256
CONV_WIDTH = 3
D_SGU = 256
SGU_GROUPS = 4
SGU_GROUP_DIM = 64
SGU_CHUNK = 128
D_IN = 2048
PEER_HEADS = 8
PEER_TOPK = 16
N_KEYS = 128
N_EXPERTS = N_KEYS * N_KEYS
PEER_HALF = 128
EPS = 1e-6
NEG_INF = -1e30
SQRT_HALF = 0.7071067811865476

V7X_VMEM_LIMIT_BYTES = 56 * 1024 * 1024
LANES = 128
SUBLANES = 8

N_EXTRACT = PEER_TOPK + 1
T_ROWS = 24
CAND_ROWS = T_ROWS + 7 * SUBLANES + (T_ROWS - SUBLANES)
KEY_PAD = SUBLANES
MXU_COLS = 256

f32 = jnp.float32
bf16 = jnp.bfloat16


def _dot(a, b):
    return jnp.dot(a, b, preferred_element_type=f32)


def _group_sum(v, bd):
    hi = v.astype(bf16)
    lo = (v - hi.astype(f32)).astype(bf16)
    return _dot(hi, bd) + _dot(lo, bd)


def _adaln_body(c_ref, w_ref, b_ref, o_ref):
    c = c_ref[...]
    c_act = c * (1.0 / (1.0 + jnp.exp(-c)))
    o_ref[...] = _dot(c_act.astype(bf16), w_ref[...].astype(bf16)) + b_ref[...]


def _adaln(c, w_ada, b_ada):
    depth, d, _ = w_ada.shape
    batch = c.shape[0]
    out = pl.pallas_call(
        _adaln_body,
        grid=(depth, 6),
        in_specs=[
            pl.BlockSpec((batch, d), lambda l, j: (0, 0)),
            pl.BlockSpec((None, d, d), lambda l, j: (l, 0, j)),
            pl.BlockSpec((None, None, 1, d), lambda l, j: (l, j, 0, 0)),
        ],
        out_specs=pl.BlockSpec((None, None, batch, d), lambda l, j: (l, j, 0, 0)),
        out_shape=jax.ShapeDtypeStruct((depth, 6, batch, d), f32),
        name="adaln",
    )(c, w_ada, b_ada.reshape(depth, 6, 1, d))
    return out.transpose(0, 2, 1, 3)


def _mixer_body(x_ref, mod_ref, n1g_ref, win_ref, qg_ref, kg_ref, sink_ref, bias_ref, convw_ref,
                sguw_ref, sgub_ref, ong_ref, wout_ref, bd_ref, o_ref, kv_carry, z_carry, *, ts):
    s_idx = pl.program_id(1)

    @pl.when(s_idx == 0)
    def _():
        kv_carry[...] = jnp.zeros_like(kv_carry)
        z_carry[...] = jnp.zeros_like(z_carry)

    x = x_ref[...]
    mod = mod_ref[...]
    sh1, sc1, g1 = mod[0:1], mod[1:2], mod[2:3]
    h = x * lax.rsqrt(jnp.mean(x * x, axis=-1, keepdims=True) + EPS) * n1g_ref[...]
    h = h * (1.0 + sc1) + sh1
    proj = _dot(h.astype(bf16), win_ref[...])

    bd = bd_ref[...]
    q = proj[:, 0:D_ATTN]
    k = proj[:, D_ATTN:D_ATTN + D_KV]
    v = proj[:, D_ATTN + D_KV:D_ATTN + 2 * D_KV]
    o0 = D_ATTN + 2 * D_KV
    cb = proj[:, o0:o0 + D_CONV]
    cc = proj[:, o0 + D_CONV:o0 + 2 * D_CONV]
    ch = proj[:, o0 + 2 * D_CONV:o0 + 3 * D_CONV]
    o1 = o0 + 3 * D_CONV
    su = proj[:, o1:o1 + D_SGU]
    sv = proj[:, o1 + D_SGU:o1 + 2 * D_SGU]

    inv_hd = 1.0 / HEAD_DIM
    qn = q * lax.rsqrt(_group_sum(q * q, bd) * inv_hd + EPS) * qg_ref[...]
    qn = (qn * (HEAD_DIM ** -0.5)).astype(bf16)
    kn = k * lax.rsqrt(_group_sum(k * k, bd[0:D_KV, 0:D_KV]) * inv_hd + EPS) * kg_ref[...]
    k_ext = jnp.concatenate([kv_carry[:, 0:D_KV], kn], axis=0).astype(bf16)
    v_ext = jnp.concatenate([kv_carry[:, D_KV:2 * D_KV], v], axis=0).astype(bf16)

    col = lax.broadcasted_iota(jnp.int32, (1, 2 * ATT_BLOCK), 1)
    no_prev = jnp.where(col < ATT_BLOCK, jnp.where(s_idx == 0, NEG_INF, 0.0), 0.0)

    attn_blocks = []
    for n in range(ts // ATT_BLOCK):
        q_blk = qn[n * ATT_BLOCK:(n + 1) * ATT_BLOCK]
        k_blk = k_ext[n * ATT_BLOCK:(n + 2) * ATT_BLOCK]
        v_blk = v_ext[n * ATT_BLOCK:(n + 2) * ATT_BLOCK]
        heads = []
        for hq in range(N_Q_HEADS):
            g = hq // GQA_GROUP
            qh = q_blk[:, hq * HEAD_DIM:(hq + 1) * HEAD_DIM]
            kh = k_blk[:, g * HEAD_DIM:(g + 1) * HEAD_DIM]
            vh = v_blk[:, g * HEAD_DIM:(g + 1) * HEAD_DIM]
            logits = lax.dot_general(qh, kh, (((1,), (1,)), ((), ())),
                                     preferred_element_type=f32) + bias_ref[hq]
            if n == 0:
                logits = logits + no_prev
            sink = sink_ref[hq]
            m = jnp.maximum(jnp.max(logits, axis=-1, keepdims=True), sink)
            p = jnp.exp(logits - m)
            denom = jnp.sum(p, axis=-1, keepdims=True) + jnp.exp(sink - m)
            heads.append(_dot(p.astype(bf16), vh) * (1.0 / denom))
        attn_blocks.append(jnp.concatenate(heads, axis=-1))
    y_attn = jnp.concatenate(attn_blocks, axis=0)

    z = cc * ch
    row = lax.broadcasted_iota(jnp.int32, (ts, D_CONV), 0)
    zc = z_carry[...]
    z1 = jnp.where(row == 0, zc[7:8], pltpu.roll(z, 1, 0))
    z2 = jnp.where(row == 0, zc[6:7], jnp.where(row == 1, zc[7:8], pltpu.roll(z, 2, 0)))
    cw = convw_ref[...]
    y_conv = cb * (z2 * cw[0:1] + z1 * cw[1:2] + z * cw[2:3])

    bd_s = bd[0:D_SGU, 0:D_SGU]
    inv_g = 1.0 / SGU_GROUP_DIM
    mu = _group_sum(sv, bd_s) * inv_g
    dv = sv - mu
    var = _group_sum(dv * dv, bd_s) * inv_g
    vn = (dv * lax.rsqrt(var + EPS)).astype(bf16)
    r_i = lax.broadcasted_iota(jnp.int32, (SGU_CHUNK, SGU_CHUNK), 0)
    c_i = lax.broadcasted_iota(jnp.int32, (SGU_CHUNK, SGU_CHUNK), 1)
    lane_grp = lax.broadcasted_iota(jnp.int32, (1, D_SGU), 1) // SGU_GROUP_DIM
    w_tril = [jnp.where(r_i >= c_i, sguw_ref[g], 0.0).astype(bf16) for g in range(SGU_GROUPS)]
    sgub = sgub_ref[...]
    mixed_chunks = []
    for n in range(ts // SGU_CHUNK):
        vn_c = vn[n * SGU_CHUNK:(n + 1) * SGU_CHUNK]
        mixed = sgub
        for g in range(SGU_GROUPS):
            mixed = mixed + jnp.where(lane_grp == g, _dot(w_tril[g], vn_c), 0.0)
        mixed_chunks.append(mixed)
    y_sgu = su * jnp.concatenate(mixed_chunks, axis=0)

    ong = ong_ref[...]

    def rms(y, gain):
        return y * lax.rsqrt(jnp.mean(y * y, axis=-1, keepdims=True) + EPS) * gain

    merged = jnp.concatenate([
        rms(y_attn, ong[:, 0:D_ATTN]),
        rms(y_conv, ong[:, D_ATTN:D_ATTN + D_CONV]),
        rms(y_sgu, ong[:, D_ATTN + D_CONV:]),
    ], axis=-1)
    o_ref[...] = x + g1 * _dot(merged.astype(bf16), wout_ref[...])

    kv_carry[:, 0:D_KV] = kn[ts - ATT_BLOCK:ts]
    kv_carry[:, D_KV:2 * D_KV] = v[ts - ATT_BLOCK:ts]
    z_carry[...] = z[ts - SUBLANES:ts]


def _mixer(x, mod, n1g, w_in, qg, kg, sink, bias_tbl, conv_w, sgu_w, sgu_bfull, ong, w_out, bd, *, ts):
    batch, seq, d = x.shape
    full = lambda shape: pl.BlockSpec(shape, lambda b, s: (0,) * len(shape))
    return pl.pallas_call(
        functools.partial(_mixer_body, ts=ts),
        grid=(batch, seq // ts),
        in_specs=[
            pl.BlockSpec((None, ts, d), lambda b, s: (b, s, 0)),
            pl.BlockSpec((None, 6, d), lambda b, s: (b, 0, 0)),
            full((1, d)),
            full((d, D_IN)),
            full((1, D_ATTN)),
            full((1, D_KV)),
            pl.BlockSpec(memory_space=pltpu.SMEM),
            full((N_Q_HEADS, ATT_BLOCK, 2 * ATT_BLOCK)),
            full((CONV_WIDTH, D_CONV)),
            full((SGU_GROUPS, SGU_CHUNK, SGU_CHUNK)),
            full((SGU_CHUNK, D_SGU)),
            full((1, d)),
            full((d, d)),
            full((D_ATTN, D_ATTN)),
        ],
        out_specs=pl.BlockSpec((None, ts, d), lambda b, s: (b, s, 0)),
        out_shape=jax.ShapeDtypeStruct(x.shape, f32),
        scratch_shapes=[
            pltpu.VMEM((ATT_BLOCK, 2 * D_KV), f32),
            pltpu.VMEM((SUBLANES, D_CONV), f32),
        ],
        compiler_params=pltpu.CompilerParams(
            dimension_semantics=("parallel", "arbitrary"),
            vmem_limit_bytes=V7X_VMEM_LIMIT_BYTES),
        name="mixer",
    )(x, mod, n1g, w_in, qg, kg, sink, bias_tbl, conv_w, sgu_w, sgu_bfull, ong, w_out, bd)


def _extract_top(s, t_ref, slot):
    t_ref[slot] = jnp.full(t_ref.shape[1:], NEG_INF, f32)
    for a in range(N_EXTRACT):
        m = jnp.max(s, axis=0, keepdims=True)
        t_ref[slot, a:a + 1, :] = m
        s = jnp.where(s == m, NEG_INF, s)


def _gate_block(a_ref, w_ref, i_base, i_off, thr_ref, e1_ref, e2_ref, *, ts, ni):
    for lt in range(ts // LANES):
        ls = slice(lt * LANES, (lt + 1) * LANES)
        for ii in range(ni):
            i = i_base + (i_off + ii)
            rows = slice(ii * N_KEYS, (ii + 1) * N_KEYS)
            g = None
            for hh in range(PEER_HEADS):
                thr_row = thr_ref[hh, pl.ds(i, 1), :][:, ls]
                e1_row = e1_ref[hh, pl.ds(i, 1), :][:, ls]
                e2 = e2_ref[hh, :, ls]
                term = jnp.where(e2 >= thr_row, e1_row * e2, 0.0)
                g = term if g is None else g + term
            a = a_ref[rows, ls]
            w_ref[rows, ls] = (g * (a * (1.0 + lax.erf(a * SQRT_HALF)))).astype(bf16)


def _peer_body(x_ref, mod_ref, n2g_ref, wpq_ref, keys_ref, down_ref, up_odd_ref, up_even_ref, o_ref,
               h2t_ref, thr_ref, e1_ref, e2_ref, t_ref, a0_ref, a1_ref, w0_ref, w1_ref, acc_ref,
               *, ts, eb):
    g_idx = pl.program_id(1)
    n_pairs = pl.num_programs(1) - 1
    ni = eb // N_KEYS

    @pl.when(g_idx == 0)
    def _():
        x = x_ref[...]
        mod = mod_ref[...]
        sh2, sc2 = mod[3:4], mod[4:5]
        h2 = x * lax.rsqrt(jnp.mean(x * x, axis=-1, keepdims=True) + EPS) * n2g_ref[...]
        h2 = h2 * (1.0 + sc2) + sh2
        h2t = h2.T.astype(bf16)
        h2t_ref[...] = h2t
        qt = _dot(wpq_ref[...], h2t).astype(bf16)
        for hh in range(PEER_HEADS):
            s_p = []
            for p in range(2):
                r = (hh * 2 + p) * PEER_HALF
                s = _dot(keys_ref[hh * 2 + p], qt[r:r + PEER_HALF])
                _extract_top(s, t_ref, p)
                s_p.append(s)
            t1 = t_ref[0]
            t2 = t_ref[1]
            m1 = t1[0:1]
            m2 = t2[0:1]
            cand = [t1[0:1] + t2]
            for a in range(1, SUBLANES):
                cand.append(t1[a:a + 1] + t2[0:SUBLANES])
            cand.append(t1[SUBLANES:T_ROWS] + t2[0:1])
            cand = jnp.concatenate(cand, axis=0)
            c = cand
            for a in range(N_EXTRACT):
                m = jnp.max(c, axis=0, keepdims=True)
                if a == PEER_TOPK - 1:
                    c16 = m
                c = jnp.where(c == m, NEG_INF, c)
            tau = 0.5 * (c16 + m)
            z = jnp.sum(jnp.where(cand >= tau, jnp.exp(cand - (m1 + m2)), 0.0), axis=0, keepdims=True)
            thr_ref[hh, KEY_PAD:KEY_PAD + N_KEYS, :] = jnp.exp((tau - m2) - s_p[0])
            e1_ref[hh, KEY_PAD:KEY_PAD + N_KEYS, :] = jnp.exp(s_p[0] - m1) * (0.5 / z)
            e2_ref[hh] = jnp.exp(s_p[1] - m2)
            for pad in (slice(0, KEY_PAD), slice(KEY_PAD + N_KEYS, N_KEYS + 2 * KEY_PAD)):
                thr_ref[hh, pad, :] = jnp.full((KEY_PAD, ts), -NEG_INF, f32)
                e1_ref[hh, pad, :] = jnp.zeros((KEY_PAD, ts), f32)
        acc_ref[...] = jnp.zeros_like(acc_ref)
        a1_ref[...] = jnp.zeros_like(a1_ref)

    gate = functools.partial(_gate_block, thr_ref=thr_ref, e1_ref=e1_ref, e2_ref=e2_ref,
                             ts=ts, ni=ni)
    h2t = h2t_ref[...]
    pair_row = g_idx * (2 * ni)

    def up_project(up_blk_ref, w_ref):
        for c in range(ts // MXU_COLS):
            cs = slice(c * MXU_COLS, (c + 1) * MXU_COLS)
            acc_ref[:, cs] += _dot(up_blk_ref[...], w_ref[:, cs])

    a0_ref[...] = _dot(down_ref[0:eb, :], h2t)
    gate(a1_ref, w1_ref, pair_row, KEY_PAD - ni)
    up_project(up_odd_ref, w1_ref)

    a1_ref[...] = _dot(down_ref[eb:2 * eb, :], h2t)
    gate(a0_ref, w0_ref, pair_row, KEY_PAD)
    up_project(up_even_ref, w0_ref)

    @pl.when(g_idx == n_pairs)
    def _():
        g2 = mod_ref[...][5:6]
        o_ref[...] = x_ref[...] + g2 * acc_ref[...].T


def _peer(x, mod, n2g, wpq_t, keys, down, up_t, *, ts, eb):
    batch, seq, d = x.shape
    n_tok = batch * seq
    tiles_per_seq = seq // ts
    n_pairs = N_EXPERTS // (2 * eb)
    assert (2 * eb // N_KEYS) % SUBLANES == 0 and eb // N_KEYS <= KEY_PAD
    x2 = x.reshape(n_tok, d)
    out = pl.pallas_call(
        functools.partial(_peer_body, ts=ts, eb=eb),
        grid=(n_tok // ts, n_pairs + 1),
        in_specs=[
            pl.BlockSpec((ts, d), lambda t, g: (t, 0)),
            pl.BlockSpec((None, 6, d), lambda t, g: (t // tiles_per_seq, 0, 0)),
            pl.BlockSpec((1, d), lambda t, g: (0, 0)),
            pl.BlockSpec(wpq_t.shape, lambda t, g: (0, 0)),
            pl.BlockSpec(keys.shape, lambda t, g: (0, 0, 0)),
            pl.BlockSpec((2 * eb, d), lambda t, g: (jnp.minimum(g, n_pairs - 1), 0)),
            pl.BlockSpec((d, eb), lambda t, g: (0, jnp.maximum(2 * g - 1, 0))),
            pl.BlockSpec((d, eb), lambda t, g: (0, jnp.minimum(2 * g, 2 * n_pairs - 1))),
        ],
        out_specs=pl.BlockSpec((ts, d), lambda t, g: (t, 0)),
        out_shape=jax.ShapeDtypeStruct((n_tok, d), f32),
        scratch_shapes=[
            pltpu.VMEM((d, ts), bf16),
            pltpu.VMEM((PEER_HEADS, N_KEYS + 2 * KEY_PAD, ts), f32),
            pltpu.VMEM((PEER_HEADS, N_KEYS + 2 * KEY_PAD, ts), f32),
            pltpu.VMEM((PEER_HEADS, N_KEYS, ts), f32),
            pltpu.VMEM((2, T_ROWS, ts), f32),
            pltpu.VMEM((eb, ts), f32),
            pltpu.VMEM((eb, ts), f32),
            pltpu.VMEM((eb, ts), bf16),
            pltpu.VMEM((eb, ts), bf16),
            pltpu.VMEM((d, ts), f32),
        ],
        compiler_params=pltpu.CompilerParams(
            dimension_semantics=("parallel", "arbitrary"),
            vmem_limit_bytes=V7X_VMEM_LIMIT_BYTES),
        name="peer",
    )(x2, mod, n2g, wpq_t, keys, down, up_t, up_t)
    return out.reshape(batch, seq, d)


def _t5_causal_bucket(dist):
    max_exact = N_BUCKETS // 2
    d = jnp.maximum(dist, 0)
    log_ratio = jnp.log(jnp.maximum(d, 1).astype(f32) / max_exact) / math.log(MAX_DISTANCE / max_exact)
    large = max_exact + (log_ratio * (N_BUCKETS - max_exact)).astype(jnp.int32)
    large = jnp.minimum(large, N_BUCKETS - 1)
    return jnp.where(d < max_exact, d, large)


def _bias_table(rel_bias):
    q_idx = jnp.arange(ATT_BLOCK)[:, None]
    s_idx = jnp.arange(2 * ATT_BLOCK)[None, :]
    dist = q_idx + ATT_BLOCK - s_idx
    in_window = (dist >= 0) & (dist < WINDOW)
    bias = rel_bias.astype(f32)[_t5_causal_bucket(dist)].transpose(2, 0, 1)
    return jnp.where(in_window[None], bias, NEG_INF)


def _forward(x, c, rel_bias, w_ada, b_ada, norm1_g, norm2_g, w_in, q_norm_g, k_norm_g, attn_sink,
             conv_w, sgu_w, sgu_b, out_norm_g, w_out, peer_wq, peer_sub_keys, peer_down, peer_up,
             *, mixer_ts, peer_ts, peer_eb):
    depth = w_ada.shape[0]
    d = x.shape[-1]
    mod = _adaln(c, w_ada, b_ada)
    bias_tbl = _bias_table(rel_bias)
    grp = jnp.arange(D_ATTN) // HEAD_DIM
    bd = (grp[:, None] == grp[None, :]).astype(bf16)
    for l in range(depth):
        sgu_bfull = jnp.repeat(sgu_b[l].T, SGU_GROUP_DIM, axis=1)
        x = _mixer(
            x, mod[l], norm1_g[l].reshape(1, d), w_in[l].astype(bf16),
            jnp.tile(q_norm_g[l], N_Q_HEADS).reshape(1, D_ATTN),
            jnp.tile(k_norm_g[l], N_KV_HEADS).reshape(1, D_KV),
            attn_sink[l], bias_tbl, conv_w[l], sgu_w[l], sgu_bfull,
            out_norm_g[l].reshape(1, d), w_out[l].astype(bf16), bd, ts=mixer_ts)
        x = _peer(
            x, mod[l], norm2_g[l].reshape(1, d), peer_wq[l].T.astype(bf16),
            peer_sub_keys[l].reshape(2 * PEER_HEADS, N_KEYS, PEER_HALF).astype(bf16),
            peer_down[l].astype(bf16), peer_up[l].T.astype(bf16), ts=peer_ts, eb=peer_eb)
    return x


def kernel(x, c, rel_bias, w_ada, b_ada, norm1_g, norm2_g, w_in, q_norm_g, k_norm_g, attn_sink, conv_w,
           sgu_w, sgu_b, out_norm_g, w_out, peer_wq, peer_sub_keys, peer_down, peer_up):
    return _forward(x, c, rel_bias, w_ada, b_ada, norm1_g, norm2_g, w_in, q_norm_g, k_norm_g, attn_sink,
                    conv_w, sgu_w, sgu_b, out_norm_g, w_out, peer_wq, peer_sub_keys, peer_down, peer_up,
                    mixer_ts=512, peer_ts=512, peer_eb=512)
```

```python
import functools
import math

import jax
import jax.numpy as jnp
from jax import lax
from jax.experimental import pallas as pl
from jax.experimental.pallas import tpu as pltpu

D_MODEL = 1024
HEAD_DIM = 64
N_Q_HEADS = 8
N_KV_HEADS = 2
GQA_GROUP = N_Q_HEADS // N_KV_HEADS
D_ATTN = N_Q_HEADS * HEAD_DIM
D_KV = N_KV_HEADS * HEAD_DIM
WINDOW = 128
ATT_BLOCK = 128
N_BUCKETS = 32
MAX_DISTANCE = 128
D_CONV = 256
CONV_WIDTH = 3
D_SGU = 256
SGU_GROUPS = 4
SGU_GROUP_DIM = 64
SGU_CHUNK = 128
D_IN = 2048
PEER_HEADS = 8
PEER_TOPK = 16
N_KEYS = 128
N_EXPERTS = N_KEYS * N_KEYS
PEER_HALF = 128
EPS = 1e-6
NEG_INF = -1e30
SQRT_HALF = 0.7071067811865476

V7X_VMEM_LIMIT_BYTES = 56 * 1024 * 1024
LANES = 128
SUBLANES = 8

N_EXTRACT = PEER_TOPK + 1
T_ROWS = 24
CAND_ROWS = T_ROWS + 7 * SUBLANES + (T_ROWS - SUBLANES)
MXU_COLS = 256

f32 = jnp.float32
bf16 = jnp.bfloat16


def _dot(a, b):
    return jnp.dot(a, b, preferred_element_type=f32)


def _group_sum(v, bd):
    hi = v.astype(bf16)
    lo = (v - hi.astype(f32)).astype(bf16)
    return _dot(hi, bd) + _dot(lo, bd)


def _adaln_body(c_ref, w_ref, b_ref, o_ref):
    c = c_ref[...]
    c_act = c * (1.0 / (1.0 + jnp.exp(-c)))
    o_ref[...] = _dot(c_act.astype(bf16), w_ref[...].astype(bf16)) + b_ref[...]


def _adaln(c, w_ada, b_ada):
    depth, d, _ = w_ada.shape
    batch = c.shape[0]
    out = pl.pallas_call(
        _adaln_body,
        grid=(depth, 6),
        in_specs=[
            pl.BlockSpec((batch, d), lambda l, j: (0, 0)),
            pl.BlockSpec((None, d, d), lambda l, j: (l, 0, j)),
            pl.BlockSpec((None, None, 1, d), lambda l, j: (l, j, 0, 0)),
        ],
        out_specs=pl.BlockSpec((None, None, batch, d), lambda l, j: (l, j, 0, 0)),
        out_shape=jax.ShapeDtypeStruct((depth, 6, batch, d), f32),
        name="adaln",
    )(c, w_ada, b_ada.reshape(depth, 6, 1, d))
    return out.transpose(0, 2, 1, 3)


def _mixer_body(x_ref, mod_ref, n1g_ref, win_ref, qg_ref, kg_ref, sink_ref, bias_ref, convw_ref,
                sguw_ref, sgub_ref, ong_ref, wout_ref, bd_ref, o_ref, kv_carry, z_carry, *, ts):
    s_idx = pl.program_id(1)

    @pl.when(s_idx == 0)
    def _():
        kv_carry[...] = jnp.zeros_like(kv_carry)
        z_carry[...] = jnp.zeros_like(z_carry)

    x = x_ref[...]
    mod = mod_ref[...]
    sh1, sc1, g1 = mod[0:1], mod[1:2], mod[2:3]
    h = x * lax.rsqrt(jnp.mean(x * x, axis=-1, keepdims=True) + EPS) * n1g_ref[...]
    h = h * (1.0 + sc1) + sh1
    proj = _dot(h.astype(bf16), win_ref[...])

    bd = bd_ref[...]
    q = proj[:, 0:D_ATTN]
    k = proj[:, D_ATTN:D_ATTN + D_KV]
    v = proj[:, D_ATTN + D_KV:D_ATTN + 2 * D_KV]
    o0 = D_ATTN + 2 * D_KV
    cb = proj[:, o0:o0 + D_CONV]
    cc = proj[:, o0 + D_CONV:o0 + 2 * D_CONV]
    ch = proj[:, o0 + 2 * D_CONV:o0 + 3 * D_CONV]
    o1 = o0 + 3 * D_CONV
    su = proj[:, o1:o1 + D_SGU]
    sv = proj[:, o1 + D_SGU:o1 + 2 * D_SGU]

    inv_hd = 1.0 / HEAD_DIM
    qn = q * lax.rsqrt(_group_sum(q * q, bd) * inv_hd + EPS) * qg_ref[...]
    qn = (qn * (HEAD_DIM ** -0.5)).astype(bf16)
    kn = k * lax.rsqrt(_group_sum(k * k, bd[0:D_KV, 0:D_KV]) * inv_hd + EPS) * kg_ref[...]
    k_ext = jnp.concatenate([kv_carry[:, 0:D_KV], kn], axis=0).astype(bf16)
    v_ext = jnp.concatenate([kv_carry[:, D_KV:2 * D_KV], v], axis=0).astype(bf16)

    col = lax.broadcasted_iota(jnp.int32, (1, 2 * ATT_BLOCK), 1)
    no_prev = jnp.where(col < ATT_BLOCK, jnp.where(s_idx == 0, NEG_INF, 0.0), 0.0)

    attn_blocks = []
    for n in range(ts // ATT_BLOCK):
        q_blk = qn[n * ATT_BLOCK:(n + 1) * ATT_BLOCK]
        k_blk = k_ext[n * ATT_BLOCK:(n + 2) * ATT_BLOCK]
        v_blk = v_ext[n * ATT_BLOCK:(n + 2) * ATT_BLOCK]
        heads = []
        for hq in range(N_Q_HEADS):
            g = hq // GQA_GROUP
            qh = q_blk[:, hq * HEAD_DIM:(hq + 1) * HEAD_DIM]
            kh = k_blk[:, g * HEAD_DIM:(g + 1) * HEAD_DIM]
            vh = v_blk[:, g * HEAD_DIM:(g + 1) * HEAD_DIM]
            logits = lax.dot_general(qh, kh, (((1,), (1,)), ((), ())),
                                     preferred_element_type=f32) + bias_ref[hq]
            if n == 0:
                logits = logits + no_prev
            sink = sink_ref[hq]
            m = jnp.maximum(jnp.max(logits, axis=-1, keepdims=True), sink)
            p = jnp.exp(logits - m)
            denom = jnp.sum(p, axis=-1, keepdims=True) + jnp.exp(sink - m)
            heads.append(_dot(p.astype(bf16), vh) * (1.0 / denom))
        attn_blocks.append(jnp.concatenate(heads, axis=-1))
    y_attn = jnp.concatenate(attn_blocks, axis=0)

    z = cc * ch
    row = lax.broadcasted_iota(jnp.int32, (ts, D_CONV), 0)
    zc = z_carry[...]
    z1 = jnp.where(row == 0, zc[7:8], pltpu.roll(z, 1, 0))
    z2 = jnp.where(row == 0, zc[6:7], jnp.where(row == 1, zc[7:8], pltpu.roll(z, 2, 0)))
    cw = convw_ref[...]
    y_conv = cb * (z2 * cw[0:1] + z1 * cw[1:2] + z * cw[2:3])

    bd_s = bd[0:D_SGU, 0:D_SGU]
    inv_g = 1.0 / SGU_GROUP_DIM
    mu = _group_sum(sv, bd_s) * inv_g
    dv = sv - mu
    var = _group_sum(dv * dv, bd_s) * inv_g
    vn = (dv * lax.rsqrt(var + EPS)).astype(bf16)
    r_i = lax.broadcasted_iota(jnp.int32, (SGU_CHUNK, SGU_CHUNK), 0)
    c_i = lax.broadcasted_iota(jnp.int32, (SGU_CHUNK, SGU_CHUNK), 1)
    lane_grp = lax.broadcasted_iota(jnp.int32, (1, D_SGU), 1) // SGU_GROUP_DIM
    w_tril = [jnp.where(r_i >= c_i, sguw_ref[g], 0.0).astype(bf16) for g in range(SGU_GROUPS)]
    sgub = sgub_ref[...]
    mixed_chunks = []
    for n in range(ts // SGU_CHUNK):
        vn_c = vn[n * SGU_CHUNK:(n + 1) * SGU_CHUNK]
        mixed = sgub
        for g in range(SGU_GROUPS):
            mixed = mixed + jnp.where(lane_grp == g, _dot(w_tril[g], vn_c), 0.0)
        mixed_chunks.append(mixed)
    y_sgu = su * jnp.concatenate(mixed_chunks, axis=0)

    ong = ong_ref[...]

    def rms(y, gain):
        return y * lax.rsqrt(jnp.mean(y * y, axis=-1, keepdims=True) + EPS) * gain

    merged = jnp.concatenate([
        rms(y_attn, ong[:, 0:D_ATTN]),
        rms(y_conv, ong[:, D_ATTN:D_ATTN + D_CONV]),
        rms(y_sgu, ong[:, D_ATTN + D_CONV:]),
    ], axis=-1)
    o_ref[...] = x + g1 * _dot(merged.astype(bf16), wout_ref[...])

    kv_carry[:, 0:D_KV] = kn[ts - ATT_BLOCK:ts]
    kv_carry[:, D_KV:2 * D_KV] = v[ts - ATT_BLOCK:ts]
    z_carry[...] = z[ts - SUBLANES:ts]


def _mixer(x, mod, n1g, w_in, qg, kg, sink, bias_tbl, conv_w, sgu_w, sgu_bfull, ong, w_out, bd, *, ts):
    batch, seq, d = x.shape
    full = lambda shape: pl.BlockSpec(shape, lambda b, s: (0,) * len(shape))
    return pl.pallas_call(
        functools.partial(_mixer_body, ts=ts),
        grid=(batch, seq // ts),
        in_specs=[
            pl.BlockSpec((None, ts, d), lambda b, s: (b, s, 0)),
            pl.BlockSpec((None, 6, d), lambda b, s: (b, 0, 0)),
            full((1, d)),
            full((d, D_IN)),
            full((1, D_ATTN)),
            full((1, D_KV)),
            pl.BlockSpec(memory_space=pltpu.SMEM),
            full((N_Q_HEADS, ATT_BLOCK, 2 * ATT_BLOCK)),
            full((CONV_WIDTH, D_CONV)),
            full((SGU_GROUPS, SGU_CHUNK, SGU_CHUNK)),
            full((SGU_CHUNK, D_SGU)),
            full((1, d)),
            full((d, d)),
            full((D_ATTN, D_ATTN)),
        ],
        out_specs=pl.BlockSpec((None, ts, d), lambda b, s: (b, s, 0)),
        out_shape=jax.ShapeDtypeStruct(x.shape, f32),
        scratch_shapes=[
            pltpu.VMEM((ATT_BLOCK, 2 * D_KV), f32),
            pltpu.VMEM((SUBLANES, D_CONV), f32),
        ],
        compiler_params=pltpu.CompilerParams(
            dimension_semantics=("parallel", "arbitrary"),
            vmem_limit_bytes=V7X_VMEM_LIMIT_BYTES),
        name="mixer",
    )(x, mod, n1g, w_in, qg, kg, sink, bias_tbl, conv_w, sgu_w, sgu_bfull, ong, w_out, bd)


def _extract_top(s, t_ref, slot):
    t_ref[slot] = jnp.full(t_ref.shape[1:], NEG_INF, f32)
    for a in range(N_EXTRACT):
        m = jnp.max(s, axis=0, keepdims=True)
        t_ref[slot, a:a + 1, :] = m
        s = jnp.where(s == m, NEG_INF, s)


def _gate_block(a_ref, w_ref, i_base, i_off, theta_ref, e1_ref, e2_ref, *, ts, ni):
    for lt in range(ts // LANES):
        ls = slice(lt * LANES, (lt + 1) * LANES)
        theta = [theta_ref[hh, 0:1, ls] for hh in range(PEER_HEADS)]
        for ii in range(ni):
            i = i_base + (i_off + ii)
            rows = slice(ii * N_KEYS, (ii + 1) * N_KEYS)
            g = None
            for hh in range(PEER_HEADS):
                e1_row = e1_ref[hh, pl.ds(i, 1), :][:, ls]
                p = e1_row * e2_ref[hh, :, ls]
                term = jnp.where(p >= theta[hh], p, 0.0)
                g = term if g is None else g + term
            a = a_ref[rows, ls]
            w_ref[rows, ls] = (g * (a * (1.0 + lax.erf(a * SQRT_HALF)))).astype(bf16)


def _peer_body(x_ref, mod_ref, n2g_ref, wpq_ref, keys_ref, down_first_ref, down_odd_ref, down_even_ref, up_ref,
               o_ref,
               h2t_ref, theta_ref, e1_ref, e2_ref, t_ref, a0_ref, a1_ref, w0_ref, w1_ref, acc_ref,
               *, ts, eb):
    g_idx = pl.program_id(1)
    ni = eb // N_KEYS

    @pl.when(g_idx == 0)
    def _():
        x = x_ref[...]
        mod = mod_ref[...]
        sh2, sc2 = mod[3:4], mod[4:5]
        h2 = x * lax.rsqrt(jnp.mean(x * x, axis=-1, keepdims=True) + EPS) * n2g_ref[...]
        h2 = h2 * (1.0 + sc2) + sh2
        h2t = h2.T.astype(bf16)
        h2t_ref[...] = h2t
        qt = _dot(wpq_ref[...], h2t).astype(bf16)
        for hh in range(PEER_HEADS):
            s_p = []
            for p in range(2):
                r = (hh * 2 + p) * PEER_HALF
                s = _dot(keys_ref[hh * 2 + p], qt[r:r + PEER_HALF])
                _extract_top(s, t_ref, p)
                s_p.append(s)
            t1 = t_ref[0]
            t2 = t_ref[1]
            m1 = t1[0:1]
            m2 = t2[0:1]
            cand = [t1[0:1] + t2]
            for a in range(1, SUBLANES):
                cand.append(t1[a:a + 1] + t2[0:SUBLANES])
            cand.append(t1[SUBLANES:T_ROWS] + t2[0:1])
            cand = jnp.concatenate(cand, axis=0)
            c = cand
            for a in range(N_EXTRACT):
                m = jnp.max(c, axis=0, keepdims=True)
                if a == PEER_TOPK - 1:
                    c16 = m
                c = jnp.where(c == m, NEG_INF, c)
            tau = 0.5 * (c16 + m)
            z = jnp.sum(jnp.where(cand >= tau, jnp.exp(cand - (m1 + m2)), 0.0), axis=0, keepdims=True)
            half_inv_z = 0.5 / z
            theta_ref[hh, 0:1, :] = jnp.exp(tau - (m1 + m2)) * half_inv_z
            e1_ref[hh] = jnp.exp(s_p[0] - m1) * half_inv_z
            e2_ref[hh] = jnp.exp(s_p[1] - m2)
        acc_ref[...] = jnp.zeros_like(acc_ref)
        a0_ref[...] = _dot(down_first_ref[...], h2t)

    gate = functools.partial(_gate_block, theta_ref=theta_ref, e1_ref=e1_ref, e2_ref=e2_ref,
                             ts=ts, ni=ni)
    h2t = h2t_ref[...]
    pair_row = g_idx * (2 * ni)

    def up_project(up_blk_ref, w_ref):
        for c in range(ts // MXU_COLS):
            cs = slice(c * MXU_COLS, (c + 1) * MXU_COLS)
            acc_ref[:, cs] += _dot(up_blk_ref[...], w_ref[:, cs])

    a1_ref[...] = _dot(down_odd_ref[...], h2t)
    gate(a0_ref, w0_ref, pair_row, 0)
    up_project(up_ref.at[:, 0:eb], w0_ref)

    a0_ref[...] = _dot(down_even_ref[...], h2t)
    gate(a1_ref, w1_ref, pair_row, ni)
    up_project(up_ref.at[:, eb:2 * eb], w1_ref)

    @pl.when(g_idx == pl.num_programs(1) - 1)
    def _():
        g2 = mod_ref[...][5:6]
        o_ref[...] = x_ref[...] + g2 * acc_ref[...].T


def _peer(x, mod, n2g, wpq_t, keys, down, up_t, *, ts, eb):
    batch, seq, d = x.shape
    n_tok = batch * seq
    tiles_per_seq = seq // ts
    n_pairs = N_EXPERTS // (2 * eb)
    assert (2 * eb // N_KEYS) % SUBLANES == 0
    resident = dict(pipeline_mode=pl.Buffered(1))
    x2 = x.reshape(n_tok, d)
    out = pl.pallas_call(
        functools.partial(_peer_body, ts=ts, eb=eb),
        grid=(n_tok // ts, n_pairs),
        in_specs=[
            pl.BlockSpec((ts, d), lambda t, g: (t, 0)),
            pl.BlockSpec((None, 6, d), lambda t, g: (t // tiles_per_seq, 0, 0)),
            pl.BlockSpec((1, d), lambda t, g: (0, 0)),
            pl.BlockSpec(wpq_t.shape, lambda t, g: (0, 0), **resident),
            pl.BlockSpec(keys.shape, lambda t, g: (0, 0, 0), **resident),
            pl.BlockSpec((eb, d), lambda t, g: (0, 0), **resident),
            pl.BlockSpec((eb, d), lambda t, g: (2 * g + 1, 0)),
            pl.BlockSpec((eb, d), lambda t, g: (jnp.minimum(2 * g + 2, 2 * n_pairs - 1), 0)),
            pl.BlockSpec((d, 2 * eb), lambda t, g: (0, g)),
        ],
        out_specs=pl.BlockSpec((ts, d), lambda t, g: (t, 0)),
        out_shape=jax.ShapeDtypeStruct((n_tok, d), f32),
        scratch_shapes=[
            pltpu.VMEM((d, ts), bf16),
            pltpu.VMEM((PEER_HEADS, SUBLANES, ts), f32),
            pltpu.VMEM((PEER_HEADS, N_KEYS, ts), f32),
            pltpu.VMEM((PEER_HEADS, N_KEYS, ts), f32),
            pltpu.VMEM((2, T_ROWS, ts), f32),
            pltpu.VMEM((eb, ts), f32),
            pltpu.VMEM((eb, ts), f32),
            pltpu.VMEM((eb, ts), bf16),
            pltpu.VMEM((eb, ts), bf16),
            pltpu.VMEM((d, ts), f32),
        ],
        compiler_params=pltpu.CompilerParams(
            dimension_semantics=("parallel", "arbitrary"),
            vmem_limit_bytes=V7X_VMEM_LIMIT_BYTES),
        name="peer",
    )(x2, mod, n2g, wpq_t, keys, down, down, down, up_t)
    return out.reshape(batch, seq, d)


def _t5_causal_bucket(dist):
    max_exact = N_BUCKETS // 2
    d = jnp.maximum(dist, 0)
    log_ratio = jnp.log(jnp.maximum(d, 1).astype(f32) / max_exact) / math.log(MAX_DISTANCE / max_exact)
    large = max_exact + (log_ratio * (N_BUCKETS - max_exact)).astype(jnp.int32)
    large = jnp.minimum(large, N_BUCKETS - 1)
    return jnp.where(d < max_exact, d, large)


def _bias_table(rel_bias):
    q_idx = jnp.arange(ATT_BLOCK)[:, None]
    s_idx = jnp.arange(2 * ATT_BLOCK)[None, :]
    dist = q_idx + ATT_BLOCK - s_idx
    in_window = (dist >= 0) & (dist < WINDOW)
    bias = rel_bias.astype(f32)[_t5_causal_bucket(dist)].transpose(2, 0, 1)
    return jnp.where(in_window[None], bias, NEG_INF)


def _forward(x, c, rel_bias, w_ada, b_ada, norm1_g, norm2_g, w_in, q_norm_g, k_norm_g, attn_sink,
             conv_w, sgu_w, sgu_b, out_norm_g, w_out, peer_wq, peer_sub_keys, peer_down, peer_up,
             *, mixer_ts, peer_ts, peer_eb):
    depth = w_ada.shape[0]
    d = x.shape[-1]
    mod = _adaln(c, w_ada, b_ada)
    bias_tbl = _bias_table(rel_bias)
    grp = jnp.arange(D_ATTN) // HEAD_DIM
    bd = (grp[:, None] == grp[None, :]).astype(bf16)
    for l in range(depth):
        sgu_bfull = jnp.repeat(sgu_b[l].T, SGU_GROUP_DIM, axis=1)
        x = _mixer(
            x, mod[l], norm1_g[l].reshape(1, d), w_in[l].astype(bf16),
            jnp.tile(q_norm_g[l], N_Q_HEADS).reshape(1, D_ATTN),
            jnp.tile(k_norm_g[l], N_KV_HEADS).reshape(1, D_KV),
            attn_sink[l], bias_tbl, conv_w[l], sgu_w[l], sgu_bfull,
            out_norm_g[l].reshape(1, d), w_out[l].astype(bf16), bd, ts=mixer_ts)
        x = _peer(
            x, mod[l], norm2_g[l].reshape(1, d), peer_wq[l].T.astype(bf16),
            peer_sub_keys[l].reshape(2 * PEER_HEADS, N_KEYS, PEER_HALF).astype(bf16),
            peer_down[l].astype(bf16), peer_up[l].T.astype(bf16), ts=peer_ts, eb=peer_eb)
    return x


def kernel(x, c, rel_bias, w_ada, b_ada, norm1_g, norm2_g, w_in, q_norm_g, k_norm_g, attn_sink, conv_w,
           sgu_w, sgu_b, out_norm_g, w_out, peer_wq, peer_sub_keys, peer_down, peer_up):
    return _forward(x, c, rel_bias, w_ada, b_ada, norm1_g, norm2_g, w_in, q_norm_g, k_norm_g, attn_sink,
                    conv_w, sgu_w, sgu_b, out_norm_g, w_out, peer_wq, peer_sub_keys, peer_down, peer_up,
                    mixer_ts=512, peer_ts=512, peer_eb=1024)
```

```python
import functools
import math

import jax
import jax.numpy as jnp
from jax import lax
from jax.experimental import pallas as pl
from jax.experimental.pallas import tpu as pltpu

D_MODEL = 1024
HEAD_DIM = 64
N_Q_HEADS = 8
N_KV_HEADS = 2
GQA_GROUP = N_Q_HEADS // N_KV_HEADS
D_ATTN = N_Q_HEADS * HEAD_DIM
D_KV = N_KV_HEADS * HEAD_DIM
WINDOW = 128
ATT_BLOCK = 128
N_BUCKETS = 32
MAX_DISTANCE = 128
D_CONV = 256
CONV_WIDTH = 3
D_SGU = 256
SGU_GROUPS = 4
SGU_GROUP_DIM = 64
SGU_CHUNK = 128
D_IN = 2048
PEER_HEADS = 8
PEER_TOPK = 16
N_KEYS = 128
N_EXPERTS = N_KEYS * N_KEYS
PEER_HALF = 128
EPS = 1e-6
NEG_INF = -1e30
SQRT_HALF = 0.7071067811865476

V7X_VMEM_LIMIT_BYTES = 56 * 1024 * 1024
LANES = 128
SUBLANES = 8

N_EXTRACT = PEER_TOPK + 1
T_ROWS = 24
CAND_ROWS = T_ROWS + 7 * SUBLANES + (T_ROWS - SUBLANES)
NO_RANK = 127.0
MXU_COLS = 256

f32 = jnp.float32
bf16 = jnp.bfloat16


def _dot(a, b):
    return jnp.dot(a, b, preferred_element_type=f32)


def _group_sum(v, bd):
    hi = v.astype(bf16)
    lo = (v - hi.astype(f32)).astype(bf16)
    return _dot(hi, bd) + _dot(lo, bd)


def _adaln_body(c_ref, w_ref, b_ref, o_ref):
    c = c_ref[...]
    c_act = c * (1.0 / (1.0 + jnp.exp(-c)))
    o_ref[...] = _dot(c_act.astype(bf16), w_ref[...].astype(bf16)) + b_ref[...]


def _adaln(c, w_ada, b_ada):
    depth, d, _ = w_ada.shape
    batch = c.shape[0]
    out = pl.pallas_call(
        _adaln_body,
        grid=(depth, 6),
        in_specs=[
            pl.BlockSpec((batch, d), lambda l, j: (0, 0)),
            pl.BlockSpec((None, d, d), lambda l, j: (l, 0, j)),
            pl.BlockSpec((None, None, 1, d), lambda l, j: (l, j, 0, 0)),
        ],
        out_specs=pl.BlockSpec((None, None, batch, d), lambda l, j: (l, j, 0, 0)),
        out_shape=jax.ShapeDtypeStruct((depth, 6, batch, d), f32),
        name="adaln",
    )(c, w_ada, b_ada.reshape(depth, 6, 1, d))
    return out.transpose(0, 2, 1, 3)


def _mixer_body(x_ref, mod_ref, n1g_ref, win_ref, qg_ref, kg_ref, sink_ref, bias_ref, convw_ref,
                sguw_ref, sgub_ref, ong_ref, wout_ref, bd_ref, o_ref, kv_carry, z_carry, *, ts):
    s_idx = pl.program_id(1)

    @pl.when(s_idx == 0)
    def _():
        kv_carry[...] = jnp.zeros_like(kv_carry)
        z_carry[...] = jnp.zeros_like(z_carry)

    x = x_ref[...]
    mod = mod_ref[...]
    sh1, sc1, g1 = mod[0:1], mod[1:2], mod[2:3]
    h = x * lax.rsqrt(jnp.mean(x * x, axis=-1, keepdims=True) + EPS) * n1g_ref[...]
    h = h * (1.0 + sc1) + sh1
    proj = _dot(h.astype(bf16), win_ref[...])

    bd = bd_ref[...]
    q = proj[:, 0:D_ATTN]
    k = proj[:, D_ATTN:D_ATTN + D_KV]
    v = proj[:, D_ATTN + D_KV:D_ATTN + 2 * D_KV]
    o0 = D_ATTN + 2 * D_KV
    cb = proj[:, o0:o0 + D_CONV]
    cc = proj[:, o0 + D_CONV:o0 + 2 * D_CONV]
    ch = proj[:, o0 + 2 * D_CONV:o0 + 3 * D_CONV]
    o1 = o0 + 3 * D_CONV
    su = proj[:, o1:o1 + D_SGU]
    sv = proj[:, o1 + D_SGU:o1 + 2 * D_SGU]

    inv_hd = 1.0 / HEAD_DIM
    qn = q * lax.rsqrt(_group_sum(q * q, bd) * inv_hd + EPS) * qg_ref[...]
    qn = (qn * (HEAD_DIM ** -0.5)).astype(bf16)
    kn = k * lax.rsqrt(_group_sum(k * k, bd[0:D_KV, 0:D_KV]) * inv_hd + EPS) * kg_ref[...]
    k_ext = jnp.concatenate([kv_carry[:, 0:D_KV], kn], axis=0).astype(bf16)
    v_ext = jnp.concatenate([kv_carry[:, D_KV:2 * D_KV], v], axis=0).astype(bf16)

    col = lax.broadcasted_iota(jnp.int32, (1, 2 * ATT_BLOCK), 1)
    no_prev = jnp.where(col < ATT_BLOCK, jnp.where(s_idx == 0, NEG_INF, 0.0), 0.0)

    attn_blocks = []
    for n in range(ts // ATT_BLOCK):
        q_blk = qn[n * ATT_BLOCK:(n + 1) * ATT_BLOCK]
        k_blk = k_ext[n * ATT_BLOCK:(n + 2) * ATT_BLOCK]
        v_blk = v_ext[n * ATT_BLOCK:(n + 2) * ATT_BLOCK]
        heads = []
        for hq in range(N_Q_HEADS):
            g = hq // GQA_GROUP
            qh = q_blk[:, hq * HEAD_DIM:(hq + 1) * HEAD_DIM]
            kh = k_blk[:, g * HEAD_DIM:(g + 1) * HEAD_DIM]
            vh = v_blk[:, g * HEAD_DIM:(g + 1) * HEAD_DIM]
            logits = lax.dot_general(qh, kh, (((1,), (1,)), ((), ())),
                                     preferred_element_type=f32) + bias_ref[hq]
            if n == 0:
                logits = logits + no_prev
            sink = sink_ref[hq]
            m = jnp.maximum(jnp.max(logits, axis=-1, keepdims=True), sink)
            p = jnp.exp(logits - m)
            denom = jnp.sum(p, axis=-1, keepdims=True) + jnp.exp(sink - m)
            heads.append(_dot(p.astype(bf16), vh) * (1.0 / denom))
        attn_blocks.append(jnp.concatenate(heads, axis=-1))
    y_attn = jnp.concatenate(attn_blocks, axis=0)

    z = cc * ch
    row = lax.broadcasted_iota(jnp.int32, (ts, D_CONV), 0)
    zc = z_carry[...]
    z1 = jnp.where(row == 0, zc[7:8], pltpu.roll(z, 1, 0))
    z2 = jnp.where(row == 0, zc[6:7], jnp.where(row == 1, zc[7:8], pltpu.roll(z, 2, 0)))
    cw = convw_ref[...]
    y_conv = cb * (z2 * cw[0:1] + z1 * cw[1:2] + z * cw[2:3])

    bd_s = bd[0:D_SGU, 0:D_SGU]
    inv_g = 1.0 / SGU_GROUP_DIM
    mu = _group_sum(sv, bd_s) * inv_g
    dv = sv - mu
    var = _group_sum(dv * dv, bd_s) * inv_g
    vn = (dv * lax.rsqrt(var + EPS)).astype(bf16)
    r_i = lax.broadcasted_iota(jnp.int32, (SGU_CHUNK, SGU_CHUNK), 0)
    c_i = lax.broadcasted_iota(jnp.int32, (SGU_CHUNK, SGU_CHUNK), 1)
    lane_grp = lax.broadcasted_iota(jnp.int32, (1, D_SGU), 1) // SGU_GROUP_DIM
    w_tril = [jnp.where(r_i >= c_i, sguw_ref[g], 0.0).astype(bf16) for g in range(SGU_GROUPS)]
    sgub = sgub_ref[...]
    mixed_chunks = []
    for n in range(ts // SGU_CHUNK):
        vn_c = vn[n * SGU_CHUNK:(n + 1) * SGU_CHUNK]
        mixed = sgub
        for g in range(SGU_GROUPS):
            mixed = mixed + jnp.where(lane_grp == g, _dot(w_tril[g], vn_c), 0.0)
        mixed_chunks.append(mixed)
    y_sgu = su * jnp.concatenate(mixed_chunks, axis=0)

    ong = ong_ref[...]

    def rms(y, gain):
        return y * lax.rsqrt(jnp.mean(y * y, axis=-1, keepdims=True) + EPS) * gain

    merged = jnp.concatenate([
        rms(y_attn, ong[:, 0:D_ATTN]),
        rms(y_conv, ong[:, D_ATTN:D_ATTN + D_CONV]),
        rms(y_sgu, ong[:, D_ATTN + D_CONV:]),
    ], axis=-1)
    o_ref[...] = x + g1 * _dot(merged.astype(bf16), wout_ref[...])

    kv_carry[:, 0:D_KV] = kn[ts - ATT_BLOCK:ts]
    kv_carry[:, D_KV:2 * D_KV] = v[ts - ATT_BLOCK:ts]
    z_carry[...] = z[ts - SUBLANES:ts]


def _mixer(x, mod, n1g, w_in, qg, kg, sink, bias_tbl, conv_w, sgu_w, sgu_bfull, ong, w_out, bd, *, ts):
    batch, seq, d = x.shape
    full = lambda shape: pl.BlockSpec(shape, lambda b, s: (0,) * len(shape))
    return pl.pallas_call(
        functools.partial(_mixer_body, ts=ts),
        grid=(batch, seq // ts),
        in_specs=[
            pl.BlockSpec((None, ts, d), lambda b, s: (b, s, 0)),
            pl.BlockSpec((None, 6, d), lambda b, s: (b, 0, 0)),
            full((1, d)),
            full((d, D_IN)),
            full((1, D_ATTN)),
            full((1, D_KV)),
            pl.BlockSpec(memory_space=pltpu.SMEM),
            full((N_Q_HEADS, ATT_BLOCK, 2 * ATT_BLOCK)),
            full((CONV_WIDTH, D_CONV)),
            full((SGU_GROUPS, SGU_CHUNK, SGU_CHUNK)),
            full((SGU_CHUNK, D_SGU)),
            full((1, d)),
            full((d, d)),
            full((D_ATTN, D_ATTN)),
        ],
        out_specs=pl.BlockSpec((None, ts, d), lambda b, s: (b, s, 0)),
        out_shape=jax.ShapeDtypeStruct(x.shape, f32),
        scratch_shapes=[
            pltpu.VMEM((ATT_BLOCK, 2 * D_KV), f32),
            pltpu.VMEM((SUBLANES, D_CONV), f32),
        ],
        compiler_params=pltpu.CompilerParams(
            dimension_semantics=("parallel", "arbitrary"),
            vmem_limit_bytes=V7X_VMEM_LIMIT_BYTES),
        name="mixer",
    )(x, mod, n1g, w_in, qg, kg, sink, bias_tbl, conv_w, sgu_w, sgu_bfull, ong, w_out, bd)


def _extract_top(s, t_ref, slot, want_rank):
    t_ref[slot] = jnp.full(t_ref.shape[1:], NEG_INF, f32)
    rank = jnp.full(s.shape, NO_RANK, f32) if want_rank else None
    for a in range(N_EXTRACT):
        m = jnp.max(s, axis=0, keepdims=True)
        t_ref[slot, a:a + 1, :] = m
        hit = s == m
        if want_rank:
            rank = jnp.where(hit, float(a), rank)
        s = jnp.where(hit, NEG_INF, s)
    return rank


def _head_tables(s1, s2, t_ref):
    _extract_top(s1, t_ref, 0, want_rank=False)
    rank2 = _extract_top(s2, t_ref, 1, want_rank=True)
    t1 = t_ref[0]
    t2 = t_ref[1]
    m1 = t1[0:1]
    m2 = t2[0:1]
    cand = [t1[0:1] + t2]
    for a in range(1, SUBLANES):
        cand.append(t1[a:a + 1] + t2[0:SUBLANES])
    cand.append(t1[SUBLANES:T_ROWS] + t2[0:1])
    cand = jnp.concatenate(cand, axis=0)
    c = cand
    for a in range(N_EXTRACT):
        m = jnp.max(c, axis=0, keepdims=True)
        if a == PEER_TOPK - 1:
            c16 = m
        c = jnp.where(c == m, NEG_INF, c)
    tau = 0.5 * (c16 + m)
    z = jnp.sum(jnp.where(cand >= tau, jnp.exp(cand - (m1 + m2)), 0.0), axis=0, keepdims=True)
    need = tau - s1
    cnt = jnp.zeros_like(need)
    for b in range(PEER_TOPK):
        cnt = cnt + jnp.where(t2[b:b + 1] >= need, 1.0, 0.0)
    return cnt, jnp.exp(s1 - m1) * (0.5 / z), jnp.exp(s2 - m2), rank2


def _gate_block(a_ref, w_ref, i_base, i_off, cnt_ref, e1_ref, e2_ref, rank2_ref, *, ts, ni):
    for lt in range(ts // LANES):
        ls = slice(lt * LANES, (lt + 1) * LANES)
        for ii in range(ni):
            i = i_base + (i_off + ii)
            rows = slice(ii * N_KEYS, (ii + 1) * N_KEYS)
            g = None
            for hh in range(PEER_HEADS):
                e1_row = e1_ref[hh, pl.ds(i, 1), :][:, ls].astype(bf16)
                cnt_row = cnt_ref[hh, pl.ds(i, 1), :][:, ls].astype(bf16)
                term = jnp.where(rank2_ref[hh, :, ls] < cnt_row, e1_row * e2_ref[hh, :, ls], 0.0)
                g = term if g is None else g + term
            a = a_ref[rows, ls]
            w_ref[rows, ls] = g * (a * (1.0 + lax.erf(a * SQRT_HALF))).astype(bf16)


def _peer_body(x_ref, mod_ref, n2g_ref, wpq_ref, keys_ref, down_first_ref, down_odd_ref, down_even_ref, up_ref,
               o_ref,
               h2t_ref, cnt_ref, e1_ref, e2_ref, rank2_ref, s_ref, t_ref, a0_ref, a1_ref, w0_ref, w1_ref, acc_ref,
               *, ts, eb):
    g_idx = pl.program_id(1)
    ni = eb // N_KEYS

    @pl.when(g_idx == 0)
    def _():
        x = x_ref[...]
        mod = mod_ref[...]
        sh2, sc2 = mod[3:4], mod[4:5]
        h2 = x * lax.rsqrt(jnp.mean(x * x, axis=-1, keepdims=True) + EPS) * n2g_ref[...]
        h2 = h2 * (1.0 + sc2) + sh2
        h2t = h2.T.astype(bf16)
        h2t_ref[...] = h2t
        qt = _dot(wpq_ref[...], h2t).astype(bf16)
        for hh in range(PEER_HEADS):
            for p in range(2):
                r = (hh * 2 + p) * PEER_HALF
                s_ref[p] = _dot(keys_ref[hh * 2 + p], qt[r:r + PEER_HALF])
            for lt in range(ts // LANES):
                ls = slice(lt * LANES, (lt + 1) * LANES)
                cnt, e1, e2, rank2 = _head_tables(s_ref[0, :, ls], s_ref[1, :, ls], t_ref)
                cnt_ref[hh, :, ls] = cnt
                e1_ref[hh, :, ls] = e1
                e2_ref[hh, :, ls] = e2.astype(bf16)
                rank2_ref[hh, :, ls] = rank2.astype(bf16)
        acc_ref[...] = jnp.zeros_like(acc_ref)
        a0_ref[...] = _dot(down_first_ref[...], h2t)

    gate = functools.partial(_gate_block, cnt_ref=cnt_ref, e1_ref=e1_ref, e2_ref=e2_ref, rank2_ref=rank2_ref,
                             ts=ts, ni=ni)
    h2t = h2t_ref[...]
    pair_row = g_idx * (2 * ni)

    def up_project(up_blk_ref, w_ref):
        for c in range(ts // MXU_COLS):
            cs = slice(c * MXU_COLS, (c + 1) * MXU_COLS)
            acc_ref[:, cs] += _dot(up_blk_ref[...], w_ref[:, cs])

    a1_ref[...] = _dot(down_odd_ref[...], h2t)
    gate(a0_ref, w0_ref, pair_row, 0)
    up_project(up_ref.at[:, 0:eb], w0_ref)

    a0_ref[...] = _dot(down_even_ref[...], h2t)
    gate(a1_ref, w1_ref, pair_row, ni)
    up_project(up_ref.at[:, eb:2 * eb], w1_ref)

    @pl.when(g_idx == pl.num_programs(1) - 1)
    def _():
        g2 = mod_ref[...][5:6]
        o_ref[...] = x_ref[...] + g2 * acc_ref[...].T


def _peer(x, mod, n2g, wpq_t, keys, down, up_t, *, ts, eb):
    batch, seq, d = x.shape
    n_tok = batch * seq
    tiles_per_seq = seq // ts
    n_pairs = N_EXPERTS // (2 * eb)
    assert (2 * eb // N_KEYS) % SUBLANES == 0
    resident = dict(pipeline_mode=pl.Buffered(1))
    x2 = x.reshape(n_tok, d)
    out = pl.pallas_call(
        functools.partial(_peer_body, ts=ts, eb=eb),
        grid=(n_tok // ts, n_pairs),
        in_specs=[
            pl.BlockSpec((ts, d), lambda t, g: (t, 0)),
            pl.BlockSpec((None, 6, d), lambda t, g: (t // tiles_per_seq, 0, 0)),
            pl.BlockSpec((1, d), lambda t, g: (0, 0)),
            pl.BlockSpec(wpq_t.shape, lambda t, g: (0, 0), **resident),
            pl.BlockSpec(keys.shape, lambda t, g: (0, 0, 0), **resident),
            pl.BlockSpec((eb, d), lambda t, g: (0, 0), **resident),
            pl.BlockSpec((eb, d), lambda t, g: (2 * g + 1, 0)),
            pl.BlockSpec((eb, d), lambda t, g: (jnp.minimum(2 * g + 2, 2 * n_pairs - 1), 0)),
            pl.BlockSpec((d, 2 * eb), lambda t, g: (0, g)),
        ],
        out_specs=pl.BlockSpec((ts, d), lambda t, g: (t, 0)),
        out_shape=jax.ShapeDtypeStruct((n_tok, d), f32),
        scratch_shapes=[
            pltpu.VMEM((d, ts), bf16),
            pltpu.VMEM((PEER_HEADS, N_KEYS, ts), f32),
            pltpu.VMEM((PEER_HEADS, N_KEYS, ts), f32),
            pltpu.VMEM((PEER_HEADS, N_KEYS, ts), bf16),
            pltpu.VMEM((PEER_HEADS, N_KEYS, ts), bf16),
            pltpu.VMEM((2, N_KEYS, ts), f32),
            pltpu.VMEM((2, T_ROWS, LANES), f32),
            pltpu.VMEM((eb, ts), f32),
            pltpu.VMEM((eb, ts), f32),
            pltpu.VMEM((eb, ts), bf16),
            pltpu.VMEM((eb, ts), bf16),
            pltpu.VMEM((d, ts), f32),
        ],
        compiler_params=pltpu.CompilerParams(
            dimension_semantics=("parallel", "arbitrary"),
            vmem_limit_bytes=V7X_VMEM_LIMIT_BYTES),
        name="peer",
    )(x2, mod, n2g, wpq_t, keys, down, down, down, up_t)
    return out.reshape(batch, seq, d)


def _t5_causal_bucket(dist):
    max_exact = N_BUCKETS // 2
    d = jnp.maximum(dist, 0)
    log_ratio = jnp.log(jnp.maximum(d, 1).astype(f32) / max_exact) / math.log(MAX_DISTANCE / max_exact)
    large = max_exact + (log_ratio * (N_BUCKETS - max_exact)).astype(jnp.int32)
    large = jnp.minimum(large, N_BUCKETS - 1)
    return jnp.where(d < max_exact, d, large)


def _bias_table(rel_bias):
    q_idx = jnp.arange(ATT_BLOCK)[:, None]
    s_idx = jnp.arange(2 * ATT_BLOCK)[None, :]
    dist = q_idx + ATT_BLOCK - s_idx
    in_window = (dist >= 0) & (dist < WINDOW)
    bias = rel_bias.astype(f32)[_t5_causal_bucket(dist)].transpose(2, 0, 1)
    return jnp.where(in_window[None], bias, NEG_INF)


def _forward(x, c, rel_bias, w_ada, b_ada, norm1_g, norm2_g, w_in, q_norm_g, k_norm_g, attn_sink,
             conv_w, sgu_w, sgu_b, out_norm_g, w_out, peer_wq, peer_sub_keys, peer_down, peer_up,
             *, mixer_ts, peer_ts, peer_eb):
    depth = w_ada.shape[0]
    d = x.shape[-1]
    mod = _adaln(c, w_ada, b_ada)
    bias_tbl = _bias_table(rel_bias)
    grp = jnp.arange(D_ATTN) // HEAD_DIM
    bd = (grp[:, None] == grp[None, :]).astype(bf16)
    for l in range(depth):
        sgu_bfull = jnp.repeat(sgu_b[l].T, SGU_GROUP_DIM, axis=1)
        x = _mixer(
            x, mod[l], norm1_g[l].reshape(1, d), w_in[l].astype(bf16),
            jnp.tile(q_norm_g[l], N_Q_HEADS).reshape(1, D_ATTN),
            jnp.tile(k_norm_g[l], N_KV_HEADS).reshape(1, D_KV),
            attn_sink[l], bias_tbl, conv_w[l], sgu_w[l], sgu_bfull,
            out_norm_g[l].reshape(1, d), w_out[l].astype(bf16), bd, ts=mixer_ts)
        x = _peer(
            x, mod[l], norm2_g[l].reshape(1, d), peer_wq[l].T.astype(bf16),
            peer_sub_keys[l].reshape(2 * PEER_HEADS, N_KEYS, PEER_HALF).astype(bf16),
            peer_down[l].astype(bf16), peer_up[l].T.astype(bf16), ts=peer_ts, eb=peer_eb)
    return x


def kernel(x, c, rel_bias, w_ada, b_ada, norm1_g, norm2_g, w_in, q_norm_g, k_norm_g, attn_sink, conv_w,
           sgu_w, sgu_b, out_norm_g, w_out, peer_wq, peer_sub_keys, peer_down, peer_up):
    return _forward(x, c, rel_bias, w_ada, b_ada, norm1_g, norm2_g, w_in, q_norm_g, k_norm_g, attn_sink,
                    conv_w, sgu_w, sgu_b, out_norm_g, w_out, peer_wq, peer_sub_keys, peer_down, peer_up,
                    mixer_ts=512, peer_ts=512, peer_eb=1024)
```

```python
import functools
import math

import jax
import jax.numpy as jnp
from jax import lax
from jax.experimental import pallas as pl
from jax.experimental.pallas import tpu as pltpu

D_MODEL = 1024
HEAD_DIM = 64
N_Q_HEADS = 8
N_KV_HEADS = 2
GQA_GROUP = N_Q_HEADS // N_KV_HEADS
D_ATTN = N_Q_HEADS * HEAD_DIM
D_KV = N_KV_HEADS * HEAD_DIM
WINDOW = 128
ATT_BLOCK = 128
N_BUCKETS = 32
MAX_DISTANCE = 128
D_CONV = 256
CONV_WIDTH = 3
D_SGU = 256
SGU_GROUPS = 4
SGU_GROUP_DIM = 64
SGU_CHUNK = 128
D_IN = 2048
PEER_HEADS = 8
PEER_TOPK = 16
N_KEYS = 128
N_EXPERTS = N_KEYS * N_KEYS
PEER_HALF = 128
EPS = 1e-6
NEG_INF = -1e30
SQRT_HALF = 0.7071067811865476

V7X_VMEM_LIMIT_BYTES = 56 * 1024 * 1024
LANES = 128
SUBLANES = 8

MIXER_TOKENS = 512
PEER_TOKENS = 512
PEER_EXPERT_BLOCK = 1024

N_EXTRACT = PEER_TOPK + 1
T_ROWS = 24
CAND_ROWS = T_ROWS + 7 * SUBLANES + (T_ROWS - SUBLANES)
MXU_COLS = 256

f32 = jnp.float32
bf16 = jnp.bfloat16


def _dot(a, b):
    return jnp.dot(a, b, preferred_element_type=f32)


def _group_sum(v, bd):
    hi = v.astype(bf16)
    lo = (v - hi.astype(f32)).astype(bf16)
    return _dot(hi, bd) + _dot(lo, bd)


def _adaln_body(c_ref, w_ref, b_ref, o_ref):
    c = c_ref[...]
    c_act = c * (1.0 / (1.0 + jnp.exp(-c)))
    o_ref[...] = _dot(c_act.astype(bf16), w_ref[...].astype(bf16)) + b_ref[...]


def _adaln(c, w_ada, b_ada):
    depth, d, _ = w_ada.shape
    batch = c.shape[0]
    out = pl.pallas_call(
        _adaln_body,
        grid=(depth, 6),
        in_specs=[
            pl.BlockSpec((batch, d), lambda l, j: (0, 0)),
            pl.BlockSpec((None, d, d), lambda l, j: (l, 0, j)),
            pl.BlockSpec((None, None, 1, d), lambda l, j: (l, j, 0, 0)),
        ],
        out_specs=pl.BlockSpec((None, None, batch, d), lambda l, j: (l, j, 0, 0)),
        out_shape=jax.ShapeDtypeStruct((depth, 6, batch, d), f32),
        name="adaln",
    )(c, w_ada, b_ada.reshape(depth, 6, 1, d))
    return out.transpose(0, 2, 1, 3)


def _mixer_body(x_ref, mod_ref, n1g_ref, win_ref, qg_ref, kg_ref, sink_ref, bias_ref, convw_ref,
                sguw_ref, sgub_ref, ong_ref, wout_ref, bd_ref, o_ref, kv_carry, z_carry, *, ts):
    s_idx = pl.program_id(1)

    @pl.when(s_idx == 0)
    def _():
        kv_carry[...] = jnp.zeros_like(kv_carry)
        z_carry[...] = jnp.zeros_like(z_carry)

    x = x_ref[...]
    mod = mod_ref[...]
    sh1, sc1, g1 = mod[0:1], mod[1:2], mod[2:3]
    h = x * lax.rsqrt(jnp.mean(x * x, axis=-1, keepdims=True) + EPS) * n1g_ref[...]
    h = h * (1.0 + sc1) + sh1
    proj = _dot(h.astype(bf16), win_ref[...])

    bd = bd_ref[...]
    q = proj[:, 0:D_ATTN]
    k = proj[:, D_ATTN:D_ATTN + D_KV]
    v = proj[:, D_ATTN + D_KV:D_ATTN + 2 * D_KV]
    o0 = D_ATTN + 2 * D_KV
    cb = proj[:, o0:o0 + D_CONV]
    cc = proj[:, o0 + D_CONV:o0 + 2 * D_CONV]
    ch = proj[:, o0 + 2 * D_CONV:o0 + 3 * D_CONV]
    o1 = o0 + 3 * D_CONV
    su = proj[:, o1:o1 + D_SGU]
    sv = proj[:, o1 + D_SGU:o1 + 2 * D_SGU]

    inv_hd = 1.0 / HEAD_DIM
    qn = q * lax.rsqrt(_group_sum(q * q, bd) * inv_hd + EPS) * qg_ref[...]
    qn = (qn * (HEAD_DIM ** -0.5)).astype(bf16)
    kn = k * lax.rsqrt(_group_sum(k * k, bd[0:D_KV, 0:D_KV]) * inv_hd + EPS) * kg_ref[...]
    k_ext = jnp.concatenate([kv_carry[:, 0:D_KV], kn], axis=0).astype(bf16)
    v_ext = jnp.concatenate([kv_carry[:, D_KV:2 * D_KV], v], axis=0).astype(bf16)

    col = lax.broadcasted_iota(jnp.int32, (1, 2 * ATT_BLOCK), 1)
    no_prev = jnp.where(col < ATT_BLOCK, jnp.where(s_idx == 0, NEG_INF, 0.0), 0.0)

    attn_blocks = []
    for n in range(ts // ATT_BLOCK):
        q_blk = qn[n * ATT_BLOCK:(n + 1) * ATT_BLOCK]
        k_blk = k_ext[n * ATT_BLOCK:(n + 2) * ATT_BLOCK]
        v_blk = v_ext[n * ATT_BLOCK:(n + 2) * ATT_BLOCK]
        heads = []
        for hq in range(N_Q_HEADS):
            g = hq // GQA_GROUP
            qh = q_blk[:, hq * HEAD_DIM:(hq + 1) * HEAD_DIM]
            kh = k_blk[:, g * HEAD_DIM:(g + 1) * HEAD_DIM]
            vh = v_blk[:, g * HEAD_DIM:(g + 1) * HEAD_DIM]
            logits = lax.dot_general(qh, kh, (((1,), (1,)), ((), ())),
                                     preferred_element_type=f32) + bias_ref[hq]
            if n == 0:
                logits = logits + no_prev
            sink = sink_ref[hq]
            m = jnp.maximum(jnp.max(logits, axis=-1, keepdims=True), sink)
            p = jnp.exp(logits - m)
            denom = jnp.sum(p, axis=-1, keepdims=True) + jnp.exp(sink - m)
            heads.append(_dot(p.astype(bf16), vh) * (1.0 / denom))
        attn_blocks.append(jnp.concatenate(heads, axis=-1))
    y_attn = jnp.concatenate(attn_blocks, axis=0)

    z = cc * ch
    row = lax.broadcasted_iota(jnp.int32, (ts, D_CONV), 0)
    zc = z_carry[...]
    z1 = jnp.where(row == 0, zc[7:8], pltpu.roll(z, 1, 0))
    z2 = jnp.where(row == 0, zc[6:7], jnp.where(row == 1, zc[7:8], pltpu.roll(z, 2, 0)))
    cw = convw_ref[...]
    y_conv = cb * (z2 * cw[0:1] + z1 * cw[1:2] + z * cw[2:3])

    bd_s = bd[0:D_SGU, 0:D_SGU]
    inv_g = 1.0 / SGU_GROUP_DIM
    mu = _group_sum(sv, bd_s) * inv_g
    dv = sv - mu
    var = _group_sum(dv * dv, bd_s) * inv_g
    vn = (dv * lax.rsqrt(var + EPS)).astype(bf16)
    r_i = lax.broadcasted_iota(jnp.int32, (SGU_CHUNK, SGU_CHUNK), 0)
    c_i = lax.broadcasted_iota(jnp.int32, (SGU_CHUNK, SGU_CHUNK), 1)
    lane_grp = lax.broadcasted_iota(jnp.int32, (1, D_SGU), 1) // SGU_GROUP_DIM
    w_tril = [jnp.where(r_i >= c_i, sguw_ref[g], 0.0).astype(bf16) for g in range(SGU_GROUPS)]
    sgub = sgub_ref[...]
    mixed_chunks = []
    for n in range(ts // SGU_CHUNK):
        vn_c = vn[n * SGU_CHUNK:(n + 1) * SGU_CHUNK]
        mixed = sgub
        for g in range(SGU_GROUPS):
            mixed = mixed + jnp.where(lane_grp == g, _dot(w_tril[g], vn_c), 0.0)
        mixed_chunks.append(mixed)
    y_sgu = su * jnp.concatenate(mixed_chunks, axis=0)

    ong = ong_ref[...]

    def rms(y, gain):
        return y * lax.rsqrt(jnp.mean(y * y, axis=-1, keepdims=True) + EPS) * gain

    merged = jnp.concatenate([
        rms(y_attn, ong[:, 0:D_ATTN]),
        rms(y_conv, ong[:, D_ATTN:D_ATTN + D_CONV]),
        rms(y_sgu, ong[:, D_ATTN + D_CONV:]),
    ], axis=-1)
    o_ref[...] = x + g1 * _dot(merged.astype(bf16), wout_ref[...])

    kv_carry[:, 0:D_KV] = kn[ts - ATT_BLOCK:ts]
    kv_carry[:, D_KV:2 * D_KV] = v[ts - ATT_BLOCK:ts]
    z_carry[...] = z[ts - SUBLANES:ts]


def _mixer(x, mod, n1g, w_in, qg, kg, sink, bias_tbl, conv_w, sgu_w, sgu_bfull, ong, w_out, bd, *, ts):
    batch, seq, d = x.shape
    full = lambda shape: pl.BlockSpec(shape, lambda b, s: (0,) * len(shape))
    return pl.pallas_call(
        functools.partial(_mixer_body, ts=ts),
        grid=(batch, seq // ts),
        in_specs=[
            pl.BlockSpec((None, ts, d), lambda b, s: (b, s, 0)),
            pl.BlockSpec((None, 6, d), lambda b, s: (b, 0, 0)),
            full((1, d)),
            full((d, D_IN)),
            full((1, D_ATTN)),
            full((1, D_KV)),
            pl.BlockSpec(memory_space=pltpu.SMEM),
            full((N_Q_HEADS, ATT_BLOCK, 2 * ATT_BLOCK)),
            full((CONV_WIDTH, D_CONV)),
            full((SGU_GROUPS, SGU_CHUNK, SGU_CHUNK)),
            full((SGU_CHUNK, D_SGU)),
            full((1, d)),
            full((d, d)),
            full((D_ATTN, D_ATTN)),
        ],
        out_specs=pl.BlockSpec((None, ts, d), lambda b, s: (b, s, 0)),
        out_shape=jax.ShapeDtypeStruct(x.shape, f32),
        scratch_shapes=[
            pltpu.VMEM((ATT_BLOCK, 2 * D_KV), f32),
            pltpu.VMEM((SUBLANES, D_CONV), f32),
        ],
        compiler_params=pltpu.CompilerParams(
            dimension_semantics=("parallel", "arbitrary"),
            vmem_limit_bytes=V7X_VMEM_LIMIT_BYTES),
        name="mixer",
    )(x, mod, n1g, w_in, qg, kg, sink, bias_tbl, conv_w, sgu_w, sgu_bfull, ong, w_out, bd)


def _sorting_network(n):
    def merge(lo, hi, r):
        step = r * 2
        if step < hi - lo:
            yield from merge(lo, hi, step)
            yield from merge(lo + r, hi, step)
            yield from ((i, i + r) for i in range(lo + r, hi - r, step))
        else:
            yield (lo, lo + r)

    def sort(lo, hi):
        if hi - lo >= 1:
            mid = lo + (hi - lo) // 2
            yield from sort(lo, mid)
            yield from sort(mid + 1, hi)
            yield from merge(lo, hi, 1)

    return list(sort(0, n - 1))


def _top_values(x, count):
    n_tiles = x.shape[0] // SUBLANES
    v = [x[k * SUBLANES:(k + 1) * SUBLANES] for k in range(n_tiles)]
    n_net = 1 << (n_tiles - 1).bit_length()
    for i, j in _sorting_network(n_net):
        if j < n_tiles:
            v[i], v[j] = jnp.maximum(v[i], v[j]), jnp.minimum(v[i], v[j])
    out = []
    for a in range(count):
        m = jnp.max(v[0], axis=0, keepdims=True)
        out.append(m)
        hit = v[0] == m
        for k in range(min(count - 1 - a, n_tiles)):
            below = v[k + 1] if k + 1 < n_tiles else NEG_INF
            v[k] = jnp.where(hit, below, v[k])
    return out


def _head_tables(s1, s2):
    t1 = _top_values(s1, N_EXTRACT)
    t2 = _top_values(s2, N_EXTRACT)
    m1, m2 = t1[0], t2[0]
    pad = jnp.full((T_ROWS - N_EXTRACT, s1.shape[1]), NEG_INF, f32)
    t1s = jnp.concatenate(t1 + [pad], axis=0)
    t2s = jnp.concatenate(t2 + [pad], axis=0)
    cand = [t1[0] + t2s]
    for a in range(1, SUBLANES):
        cand.append(t1[a] + t2s[0:SUBLANES])
    cand.append(t1s[SUBLANES:T_ROWS] + t2[0])
    cand = jnp.concatenate(cand, axis=0)
    top = _top_values(cand, N_EXTRACT)
    tau = 0.5 * (top[PEER_TOPK - 1] + top[PEER_TOPK])
    z = jnp.sum(jnp.where(cand >= tau, jnp.exp(cand - (m1 + m2)), 0.0), axis=0, keepdims=True)
    half_inv_z = 0.5 / z
    return jnp.exp(tau - (m1 + m2)) * half_inv_z, jnp.exp(s1 - m1) * half_inv_z, jnp.exp(s2 - m2)


def _gate_block(a_ref, w_ref, i_base, i_off, theta_ref, e1_ref, e2_ref, *, ts, ni):
    for lt in range(ts // LANES):
        ls = slice(lt * LANES, (lt + 1) * LANES)
        theta = [theta_ref[hh, 0:1, ls] for hh in range(PEER_HEADS)]
        for ii in range(ni):
            i = i_base + (i_off + ii)
            rows = slice(ii * N_KEYS, (ii + 1) * N_KEYS)
            g = None
            for hh in range(PEER_HEADS):
                e1_row = e1_ref[hh, pl.ds(i, 1), :][:, ls]
                p = e1_row * e2_ref[hh, :, ls]
                term = jnp.where(p >= theta[hh], p, 0.0)
                g = term if g is None else g + term
            a = a_ref[rows, ls]
            w_ref[rows, ls] = (g * (a * (1.0 + lax.erf(a * SQRT_HALF)))).astype(bf16)


def _peer_body(x_ref, mod_ref, n2g_ref, wpq_ref, keys_ref, down_first_ref, down_odd_ref, down_even_ref, up_ref,
               o_ref,
               h2t_ref, theta_ref, e1_ref, e2_ref, s_ref, a0_ref, a1_ref, w0_ref, w1_ref, acc_ref,
               *, ts, eb):
    g_idx = pl.program_id(1)
    ni = eb // N_KEYS

    @pl.when(g_idx == 0)
    def _():
        x = x_ref[...]
        mod = mod_ref[...]
        sh2, sc2 = mod[3:4], mod[4:5]
        h2 = x * lax.rsqrt(jnp.mean(x * x, axis=-1, keepdims=True) + EPS) * n2g_ref[...]
        h2 = h2 * (1.0 + sc2) + sh2
        h2t = h2.T.astype(bf16)
        h2t_ref[...] = h2t
        qt = _dot(wpq_ref[...], h2t).astype(bf16)
        for hh in range(PEER_HEADS):
            for p in range(2):
                r = (hh * 2 + p) * PEER_HALF
                s_ref[p] = _dot(keys_ref[hh * 2 + p], qt[r:r + PEER_HALF])
            for lt in range(ts // LANES):
                ls = slice(lt * LANES, (lt + 1) * LANES)
                theta, e1, e2 = _head_tables(s_ref[0, :, ls], s_ref[1, :, ls])
                theta_ref[hh, 0:1, ls] = theta
                e1_ref[hh, :, ls] = e1
                e2_ref[hh, :, ls] = e2
        acc_ref[...] = jnp.zeros_like(acc_ref)
        a0_ref[...] = _dot(down_first_ref[...], h2t)

    gate = functools.partial(_gate_block, theta_ref=theta_ref, e1_ref=e1_ref, e2_ref=e2_ref,
                             ts=ts, ni=ni)
    h2t = h2t_ref[...]
    pair_row = g_idx * (2 * ni)

    def up_project(up_blk_ref, w_ref):
        for c in range(ts // MXU_COLS):
            cs = slice(c * MXU_COLS, (c + 1) * MXU_COLS)
            acc_ref[:, cs] += _dot(up_blk_ref[...], w_ref[:, cs])

    a1_ref[...] = _dot(down_odd_ref[...], h2t)
    gate(a0_ref, w0_ref, pair_row, 0)
    up_project(up_ref.at[:, 0:eb], w0_ref)

    a0_ref[...] = _dot(down_even_ref[...], h2t)
    gate(a1_ref, w1_ref, pair_row, ni)
    up_project(up_ref.at[:, eb:2 * eb], w1_ref)

    @pl.when(g_idx == pl.num_programs(1) - 1)
    def _():
        g2 = mod_ref[...][5:6]
        o_ref[...] = x_ref[...] + g2 * acc_ref[...].T


def _peer(x, mod, n2g, wpq_t, keys, down, up_t, *, ts, eb):
    batch, seq, d = x.shape
    n_tok = batch * seq
    tiles_per_seq = seq // ts
    n_pairs = N_EXPERTS // (2 * eb)
    assert (2 * eb // N_KEYS) % SUBLANES == 0
    resident = dict(pipeline_mode=pl.Buffered(1))
    x2 = x.reshape(n_tok, d)
    out = pl.pallas_call(
        functools.partial(_peer_body, ts=ts, eb=eb),
        grid=(n_tok // ts, n_pairs),
        in_specs=[
            pl.BlockSpec((ts, d), lambda t, g: (t, 0)),
            pl.BlockSpec((None, 6, d), lambda t, g: (t // tiles_per_seq, 0, 0)),
            pl.BlockSpec((1, d), lambda t, g: (0, 0)),
            pl.BlockSpec(wpq_t.shape, lambda t, g: (0, 0), **resident),
            pl.BlockSpec(keys.shape, lambda t, g: (0, 0, 0), **resident),
            pl.BlockSpec((eb, d), lambda t, g: (0, 0), **resident),
            pl.BlockSpec((eb, d), lambda t, g: (2 * g + 1, 0)),
            pl.BlockSpec((eb, d), lambda t, g: (jnp.minimum(2 * g + 2, 2 * n_pairs - 1), 0)),
            pl.BlockSpec((d, 2 * eb), lambda t, g: (0, g)),
        ],
        out_specs=pl.BlockSpec((ts, d), lambda t, g: (t, 0)),
        out_shape=jax.ShapeDtypeStruct((n_tok, d), f32),
        scratch_shapes=[
            pltpu.VMEM((d, ts), bf16),
            pltpu.VMEM((PEER_HEADS, SUBLANES, ts), f32),
            pltpu.VMEM((PEER_HEADS, N_KEYS, ts), f32),
            pltpu.VMEM((PEER_HEADS, N_KEYS, ts), f32),
            pltpu.VMEM((2, N_KEYS, ts), f32),
            pltpu.VMEM((eb, ts), f32),
            pltpu.VMEM((eb, ts), f32),
            pltpu.VMEM((eb, ts), bf16),
            pltpu.VMEM((eb, ts), bf16),
            pltpu.VMEM((d, ts), f32),
        ],
        compiler_params=pltpu.CompilerParams(
            dimension_semantics=("parallel", "arbitrary"),
            vmem_limit_bytes=V7X_VMEM_LIMIT_BYTES),
        name="peer",
    )(x2, mod, n2g, wpq_t, keys, down, down, down, up_t)
    return out.reshape(batch, seq, d)


def _t5_causal_bucket(dist):
    max_exact = N_BUCKETS // 2
    d = jnp.maximum(dist, 0)
    log_ratio = jnp.log(jnp.maximum(d, 1).astype(f32) / max_exact) / math.log(MAX_DISTANCE / max_exact)
    large = max_exact + (log_ratio * (N_BUCKETS - max_exact)).astype(jnp.int32)
    large = jnp.minimum(large, N_BUCKETS - 1)
    return jnp.where(d < max_exact, d, large)


def _bias_table(rel_bias):
    q_idx = jnp.arange(ATT_BLOCK)[:, None]
    s_idx = jnp.arange(2 * ATT_BLOCK)[None, :]
    dist = q_idx + ATT_BLOCK - s_idx
    in_window = (dist >= 0) & (dist < WINDOW)
    bias = rel_bias.astype(f32)[_t5_causal_bucket(dist)].transpose(2, 0, 1)
    return jnp.where(in_window[None], bias, NEG_INF)


def _forward(x, c, rel_bias, w_ada, b_ada, norm1_g, norm2_g, w_in, q_norm_g, k_norm_g, attn_sink,
             conv_w, sgu_w, sgu_b, out_norm_g, w_out, peer_wq, peer_sub_keys, peer_down, peer_up,
             *, mixer_ts, peer_ts, peer_eb):
    depth = w_ada.shape[0]
    d = x.shape[-1]
    mod = _adaln(c, w_ada, b_ada)
    bias_tbl = _bias_table(rel_bias)
    grp = jnp.arange(D_ATTN) // HEAD_DIM
    bd = (grp[:, None] == grp[None, :]).astype(bf16)
    for l in range(depth):
        sgu_bfull = jnp.repeat(sgu_b[l].T, SGU_GROUP_DIM, axis=1)
        x = _mixer(
            x, mod[l], norm1_g[l].reshape(1, d), w_in[l].astype(bf16),
            jnp.tile(q_norm_g[l], N_Q_HEADS).reshape(1, D_ATTN),
            jnp.tile(k_norm_g[l], N_KV_HEADS).reshape(1, D_KV),
            attn_sink[l], bias_tbl, conv_w[l], sgu_w[l], sgu_bfull,
            out_norm_g[l].reshape(1, d), w_out[l].astype(bf16), bd, ts=mixer_ts)
        x = _peer(
            x, mod[l], norm2_g[l].reshape(1, d), peer_wq[l].T.astype(bf16),
            peer_sub_keys[l].reshape(2 * PEER_HEADS, N_KEYS, PEER_HALF).astype(bf16),
            peer_down[l].astype(bf16), peer_up[l].T.astype(bf16), ts=peer_ts, eb=peer_eb)
    return x


def kernel(x, c, rel_bias, w_ada, b_ada, norm1_g, norm2_g, w_in, q_norm_g, k_norm_g, attn_sink, conv_w,
           sgu_w, sgu_b, out_norm_g, w_out, peer_wq, peer_sub_keys, peer_down, peer_up):
    return _forward(x, c, rel_bias, w_ada, b_ada, norm1_g, norm2_g, w_in, q_norm_g, k_norm_g, attn_sink,
                    conv_w, sgu_w, sgu_b, out_norm_g, w_out, peer_wq, peer_sub_keys, peer_down, peer_up,
                    mixer_ts=MIXER_TOKENS, peer_ts=PEER_TOKENS, peer_eb=PEER_EXPERT_BLOCK)
```

```python
import functools
import math

import jax
import jax.numpy as jnp
from jax import lax
from jax.experimental import pallas as pl
from jax.experimental.pallas import tpu as pltpu

D_MODEL = 1024
HEAD_DIM = 64
N_Q_HEADS = 8
N_KV_HEADS = 2
GQA_GROUP = N_Q_HEADS // N_KV_HEADS
D_ATTN = N_Q_HEADS * HEAD_DIM
D_KV = N_KV_HEADS * HEAD_DIM
WINDOW = 128
ATT_BLOCK = 128
N_BUCKETS = 32
MAX_DISTANCE = 128
D_CONV = 256
CONV_WIDTH = 3
D_SGU = 256
SGU_GROUPS = 4
SGU_GROUP_DIM = 64
SGU_CHUNK = 128
D_IN = 2048
PEER_HEADS = 8
PEER_TOPK = 16
N_KEYS = 128
N_EXPERTS = N_KEYS * N_KEYS
PEER_HALF = 128
EPS = 1e-6
NEG_INF = -1e30
SQRT_HALF = 0.7071067811865476

V7X_VMEM_LIMIT_BYTES = 56 * 1024 * 1024
LANES = 128
SUBLANES = 8

MIXER_TOKENS = 512
PEER_TOKENS = 512
PEER_EXPERT_BLOCK = 1024

N_EXTRACT = PEER_TOPK + 1
T_ROWS = 24
CAND_ROWS = T_ROWS + 7 * SUBLANES + (T_ROWS - SUBLANES)
MXU_COLS = 256

f32 = jnp.float32
bf16 = jnp.bfloat16


def _dot(a, b):
    return jnp.dot(a, b, preferred_element_type=f32)


def _group_sum(v, bd):
    hi = v.astype(bf16)
    lo = (v - hi.astype(f32)).astype(bf16)
    return _dot(hi, bd) + _dot(lo, bd)


def _adaln_body(c_ref, w_ref, b_ref, o_ref):
    c = c_ref[...]
    c_act = c * (1.0 / (1.0 + jnp.exp(-c)))
    o_ref[...] = _dot(c_act.astype(bf16), w_ref[...].astype(bf16)) + b_ref[...]


def _adaln(c, w_ada, b_ada):
    depth, d, _ = w_ada.shape
    batch = c.shape[0]
    out = pl.pallas_call(
        _adaln_body,
        grid=(depth, 6),
        in_specs=[
            pl.BlockSpec((batch, d), lambda l, j: (0, 0)),
            pl.BlockSpec((None, d, d), lambda l, j: (l, 0, j)),
            pl.BlockSpec((None, None, 1, d), lambda l, j: (l, j, 0, 0)),
        ],
        out_specs=pl.BlockSpec((None, None, batch, d), lambda l, j: (l, j, 0, 0)),
        out_shape=jax.ShapeDtypeStruct((depth, 6, batch, d), f32),
        name="adaln",
    )(c, w_ada, b_ada.reshape(depth, 6, 1, d))
    return out.transpose(0, 2, 1, 3)


def _mixer_body(x_ref, mod_ref, n1g_ref, win_ref, qg_ref, kg_ref, sink_ref, bias_ref, convw_ref,
                sguw_ref, sgub_ref, ong_ref, wout_ref, bd_ref, o_ref, kv_carry, z_carry, *, ts):
    s_idx = pl.program_id(1)

    @pl.when(s_idx == 0)
    def _():
        kv_carry[...] = jnp.zeros_like(kv_carry)
        z_carry[...] = jnp.zeros_like(z_carry)

    x = x_ref[...]
    mod = mod_ref[...]
    sh1, sc1, g1 = mod[0:1], mod[1:2], mod[2:3]
    h = x * lax.rsqrt(jnp.mean(x * x, axis=-1, keepdims=True) + EPS) * n1g_ref[...]
    h = h * (1.0 + sc1) + sh1
    proj = _dot(h.astype(bf16), win_ref[...])

    bd = bd_ref[...]
    q = proj[:, 0:D_ATTN]
    k = proj[:, D_ATTN:D_ATTN + D_KV]
    v = proj[:, D_ATTN + D_KV:D_ATTN + 2 * D_KV]
    o0 = D_ATTN + 2 * D_KV
    cb = proj[:, o0:o0 + D_CONV]
    cc = proj[:, o0 + D_CONV:o0 + 2 * D_CONV]
    ch = proj[:, o0 + 2 * D_CONV:o0 + 3 * D_CONV]
    o1 = o0 + 3 * D_CONV
    su = proj[:, o1:o1 + D_SGU]
    sv = proj[:, o1 + D_SGU:o1 + 2 * D_SGU]

    inv_hd = 1.0 / HEAD_DIM
    qn = q * lax.rsqrt(_group_sum(q * q, bd) * inv_hd + EPS) * qg_ref[...]
    qn = (qn * (HEAD_DIM ** -0.5)).astype(bf16)
    kn = k * lax.rsqrt(_group_sum(k * k, bd[0:D_KV, 0:D_KV]) * inv_hd + EPS) * kg_ref[...]
    k_ext = jnp.concatenate([kv_carry[:, 0:D_KV], kn], axis=0).astype(bf16)
    v_ext = jnp.concatenate([kv_carry[:, D_KV:2 * D_KV], v], axis=0).astype(bf16)

    col = lax.broadcasted_iota(jnp.int32, (1, 2 * ATT_BLOCK), 1)
    no_prev = jnp.where(col < ATT_BLOCK, jnp.where(s_idx == 0, NEG_INF, 0.0), 0.0)

    attn_blocks = []
    for n in range(ts // ATT_BLOCK):
        q_blk = qn[n * ATT_BLOCK:(n + 1) * ATT_BLOCK]
        k_blk = k_ext[n * ATT_BLOCK:(n + 2) * ATT_BLOCK]
        v_blk = v_ext[n * ATT_BLOCK:(n + 2) * ATT_BLOCK]
        heads = []
        for hq in range(N_Q_HEADS):
            g = hq // GQA_GROUP
            qh = q_blk[:, hq * HEAD_DIM:(hq + 1) * HEAD_DIM]
            kh = k_blk[:, g * HEAD_DIM:(g + 1) * HEAD_DIM]
            vh = v_blk[:, g * HEAD_DIM:(g + 1) * HEAD_DIM]
            logits = lax.dot_general(qh, kh, (((1,), (1,)), ((), ())),
                                     preferred_element_type=f32) + bias_ref[hq]
            if n == 0:
                logits = logits + no_prev
            sink = sink_ref[hq]
            m = jnp.maximum(jnp.max(logits, axis=-1, keepdims=True), sink)
            p = jnp.exp(logits - m)
            denom = jnp.sum(p, axis=-1, keepdims=True) + jnp.exp(sink - m)
            heads.append(_dot(p.astype(bf16), vh) * (1.0 / denom))
        attn_blocks.append(jnp.concatenate(heads, axis=-1))
    y_attn = jnp.concatenate(attn_blocks, axis=0)

    z = cc * ch
    row = lax.broadcasted_iota(jnp.int32, (ts, D_CONV), 0)
    zc = z_carry[...]
    z1 = jnp.where(row == 0, zc[7:8], pltpu.roll(z, 1, 0))
    z2 = jnp.where(row == 0, zc[6:7], jnp.where(row == 1, zc[7:8], pltpu.roll(z, 2, 0)))
    cw = convw_ref[...]
    y_conv = cb * (z2 * cw[0:1] + z1 * cw[1:2] + z * cw[2:3])

    bd_s = bd[0:D_SGU, 0:D_SGU]
    inv_g = 1.0 / SGU_GROUP_DIM
    mu = _group_sum(sv, bd_s) * inv_g
    dv = sv - mu
    var = _group_sum(dv * dv, bd_s) * inv_g
    vn = (dv * lax.rsqrt(var + EPS)).astype(bf16)
    r_i = lax.broadcasted_iota(jnp.int32, (SGU_CHUNK, SGU_CHUNK), 0)
    c_i = lax.broadcasted_iota(jnp.int32, (SGU_CHUNK, SGU_CHUNK), 1)
    lane_grp = lax.broadcasted_iota(jnp.int32, (1, D_SGU), 1) // SGU_GROUP_DIM
    w_tril = [jnp.where(r_i >= c_i, sguw_ref[g], 0.0).astype(bf16) for g in range(SGU_GROUPS)]
    sgub = sgub_ref[...]
    mixed_chunks = []
    for n in range(ts // SGU_CHUNK):
        vn_c = vn[n * SGU_CHUNK:(n + 1) * SGU_CHUNK]
        mixed = sgub
        for g in range(SGU_GROUPS):
            mixed = mixed + jnp.where(lane_grp == g, _dot(w_tril[g], vn_c), 0.0)
        mixed_chunks.append(mixed)
    y_sgu = su * jnp.concatenate(mixed_chunks, axis=0)

    ong = ong_ref[...]

    def rms(y, gain):
        return y * lax.rsqrt(jnp.mean(y * y, axis=-1, keepdims=True) + EPS) * gain

    merged = jnp.concatenate([
        rms(y_attn, ong[:, 0:D_ATTN]),
        rms(y_conv, ong[:, D_ATTN:D_ATTN + D_CONV]),
        rms(y_sgu, ong[:, D_ATTN + D_CONV:]),
    ], axis=-1)
    o_ref[...] = x + g1 * _dot(merged.astype(bf16), wout_ref[...])

    kv_carry[:, 0:D_KV] = kn[ts - ATT_BLOCK:ts]
    kv_carry[:, D_KV:2 * D_KV] = v[ts - ATT_BLOCK:ts]
    z_carry[...] = z[ts - SUBLANES:ts]


def _mixer(x, mod, n1g, w_in, qg, kg, sink, bias_tbl, conv_w, sgu_w, sgu_bfull, ong, w_out, bd, *, ts):
    batch, seq, d = x.shape
    full = lambda shape: pl.BlockSpec(shape, lambda b, s: (0,) * len(shape))
    return pl.pallas_call(
        functools.partial(_mixer_body, ts=ts),
        grid=(batch, seq // ts),
        in_specs=[
            pl.BlockSpec((None, ts, d), lambda b, s: (b, s, 0)),
            pl.BlockSpec((None, 6, d), lambda b, s: (b, 0, 0)),
            full((1, d)),
            full((d, D_IN)),
            full((1, D_ATTN)),
            full((1, D_KV)),
            pl.BlockSpec(memory_space=pltpu.SMEM),
            full((N_Q_HEADS, ATT_BLOCK, 2 * ATT_BLOCK)),
            full((CONV_WIDTH, D_CONV)),
            full((SGU_GROUPS, SGU_CHUNK, SGU_CHUNK)),
            full((SGU_CHUNK, D_SGU)),
            full((1, d)),
            full((d, d)),
            full((D_ATTN, D_ATTN)),
        ],
        out_specs=pl.BlockSpec((None, ts, d), lambda b, s: (b, s, 0)),
        out_shape=jax.ShapeDtypeStruct(x.shape, f32),
        scratch_shapes=[
            pltpu.VMEM((ATT_BLOCK, 2 * D_KV), f32),
            pltpu.VMEM((SUBLANES, D_CONV), f32),
        ],
        compiler_params=pltpu.CompilerParams(
            dimension_semantics=("parallel", "arbitrary"),
            vmem_limit_bytes=V7X_VMEM_LIMIT_BYTES),
        name="mixer",
    )(x, mod, n1g, w_in, qg, kg, sink, bias_tbl, conv_w, sgu_w, sgu_bfull, ong, w_out, bd)


def _sorting_network(n):
    def merge(lo, hi, r):
        step = r * 2
        if step < hi - lo:
            yield from merge(lo, hi, step)
            yield from merge(lo + r, hi, step)
            yield from ((i, i + r) for i in range(lo + r, hi - r, step))
        else:
            yield (lo, lo + r)

    def sort(lo, hi):
        if hi - lo >= 1:
            mid = lo + (hi - lo) // 2
            yield from sort(lo, mid)
            yield from sort(mid + 1, hi)
            yield from merge(lo, hi, 1)

    return list(sort(0, n - 1))


def _top_values(x, count):
    n_tiles = x.shape[0] // SUBLANES
    v = [x[k * SUBLANES:(k + 1) * SUBLANES] for k in range(n_tiles)]
    n_net = 1 << (n_tiles - 1).bit_length()
    for i, j in _sorting_network(n_net):
        if j < n_tiles:
            v[i], v[j] = jnp.maximum(v[i], v[j]), jnp.minimum(v[i], v[j])
    out = []
    for a in range(count):
        m = jnp.max(v[0], axis=0, keepdims=True)
        out.append(m)
        hit = v[0] == m
        for k in range(min(count - 1 - a, n_tiles)):
            below = v[k + 1] if k + 1 < n_tiles else NEG_INF
            v[k] = jnp.where(hit, below, v[k])
    return out


def _head_tables(s1, s2):
    t1 = _top_values(s1, N_EXTRACT)
    t2 = _top_values(s2, N_EXTRACT)
    m1, m2 = t1[0], t2[0]
    pad = jnp.full((T_ROWS - N_EXTRACT, s1.shape[1]), NEG_INF, f32)
    t1s = jnp.concatenate(t1 + [pad], axis=0)
    t2s = jnp.concatenate(t2 + [pad], axis=0)
    cand = [t1[0] + t2s]
    for a in range(1, SUBLANES):
        cand.append(t1[a] + t2s[0:SUBLANES])
    cand.append(t1s[SUBLANES:T_ROWS] + t2[0])
    cand = jnp.concatenate(cand, axis=0)
    top = _top_values(cand, N_EXTRACT)
    tau = 0.5 * (top[PEER_TOPK - 1] + top[PEER_TOPK])
    z = jnp.sum(jnp.where(cand >= tau, jnp.exp(cand - (m1 + m2)), 0.0), axis=0, keepdims=True)
    half_inv_z = SQRT_HALF / z
    return jnp.exp(tau - (m1 + m2)) * half_inv_z, jnp.exp(s1 - m1) * half_inv_z, jnp.exp(s2 - m2)


def _gate_block(a_ref, w_ref, i_base, i_off, theta_ref, e1_ref, e2_ref, *, ts, ni):
    for lt in range(ts // LANES):
        ls = slice(lt * LANES, (lt + 1) * LANES)
        theta = [theta_ref[hh, 0:1, ls] for hh in range(PEER_HEADS)]
        for ii in range(ni):
            i = i_base + (i_off + ii)
            rows = slice(ii * N_KEYS, (ii + 1) * N_KEYS)
            g = None
            for hh in range(PEER_HEADS):
                e1_row = e1_ref[hh, pl.ds(i, 1), :][:, ls]
                p = e1_row * e2_ref[hh, :, ls]
                term = jnp.where(p >= theta[hh], p, 0.0)
                g = term if g is None else g + term
            a = a_ref[rows, ls]
            w_ref[rows, ls] = (g * (a * (1.0 + lax.erf(a)))).astype(bf16)


def _peer_body(x_ref, mod_ref, n2g_ref, wpq_ref, keys_ref, down_first_ref, down_odd_ref, down_even_ref, up_ref,
               o_ref,
               h2t_ref, theta_ref, e1_ref, e2_ref, s_ref, a0_ref, a1_ref, w0_ref, w1_ref, acc_ref,
               *, ts, eb):
    g_idx = pl.program_id(1)
    ni = eb // N_KEYS

    @pl.when(g_idx == 0)
    def _():
        x = x_ref[...]
        mod = mod_ref[...]
        sh2, sc2 = mod[3:4], mod[4:5]
        h2 = x * lax.rsqrt(jnp.mean(x * x, axis=-1, keepdims=True) + EPS) * n2g_ref[...]
        h2 = h2 * (1.0 + sc2) + sh2
        h2t = h2.T.astype(bf16)
        h2t_ref[...] = h2t
        qt = _dot(wpq_ref[...], h2t).astype(bf16)
        for hh in range(PEER_HEADS):
            for p in range(2):
                r = (hh * 2 + p) * PEER_HALF
                s_ref[p] = _dot(keys_ref[hh * 2 + p], qt[r:r + PEER_HALF])
            for lt in range(ts // LANES):
                ls = slice(lt * LANES, (lt + 1) * LANES)
                theta, e1, e2 = _head_tables(s_ref[0, :, ls], s_ref[1, :, ls])
                theta_ref[hh, 0:1, ls] = theta
                e1_ref[hh, :, ls] = e1
                e2_ref[hh, :, ls] = e2
        acc_ref[...] = jnp.zeros_like(acc_ref)
        a0_ref[...] = _dot(down_first_ref[...], h2t)

    gate = functools.partial(_gate_block, theta_ref=theta_ref, e1_ref=e1_ref, e2_ref=e2_ref,
                             ts=ts, ni=ni)
    h2t = h2t_ref[...]
    pair_row = g_idx * (2 * ni)

    def up_project(up_blk_ref, w_ref):
        for c in range(ts // MXU_COLS):
            cs = slice(c * MXU_COLS, (c + 1) * MXU_COLS)
            acc_ref[:, cs] += _dot(up_blk_ref[...], w_ref[:, cs])

    a1_ref[...] = _dot(down_odd_ref[...], h2t)
    gate(a0_ref, w0_ref, pair_row, 0)
    up_project(up_ref.at[:, 0:eb], w0_ref)

    a0_ref[...] = _dot(down_even_ref[...], h2t)
    gate(a1_ref, w1_ref, pair_row, ni)
    up_project(up_ref.at[:, eb:2 * eb], w1_ref)

    @pl.when(g_idx == pl.num_programs(1) - 1)
    def _():
        g2 = mod_ref[...][5:6]
        o_ref[...] = x_ref[...] + g2 * acc_ref[...].T


def _peer(x, mod, n2g, wpq_t, keys, down, up_t, *, ts, eb):
    batch, seq, d = x.shape
    n_tok = batch * seq
    tiles_per_seq = seq // ts
    n_pairs = N_EXPERTS // (2 * eb)
    assert (2 * eb // N_KEYS) % SUBLANES == 0
    resident = dict(pipeline_mode=pl.Buffered(1))
    x2 = x.reshape(n_tok, d)
    out = pl.pallas_call(
        functools.partial(_peer_body, ts=ts, eb=eb),
        grid=(n_tok // ts, n_pairs),
        in_specs=[
            pl.BlockSpec((ts, d), lambda t, g: (t, 0)),
            pl.BlockSpec((None, 6, d), lambda t, g: (t // tiles_per_seq, 0, 0)),
            pl.BlockSpec((1, d), lambda t, g: (0, 0)),
            pl.BlockSpec(wpq_t.shape, lambda t, g: (0, 0), **resident),
            pl.BlockSpec(keys.shape, lambda t, g: (0, 0, 0), **resident),
            pl.BlockSpec((eb, d), lambda t, g: (0, 0), **resident),
            pl.BlockSpec((eb, d), lambda t, g: (2 * g + 1, 0)),
            pl.BlockSpec((eb, d), lambda t, g: (jnp.minimum(2 * g + 2, 2 * n_pairs - 1), 0)),
            pl.BlockSpec((d, 2 * eb), lambda t, g: (0, g)),
        ],
        out_specs=pl.BlockSpec((ts, d), lambda t, g: (t, 0)),
        out_shape=jax.ShapeDtypeStruct((n_tok, d), f32),
        scratch_shapes=[
            pltpu.VMEM((d, ts), bf16),
            pltpu.VMEM((PEER_HEADS, SUBLANES, ts), f32),
            pltpu.VMEM((PEER_HEADS, N_KEYS, ts), f32),
            pltpu.VMEM((PEER_HEADS, N_KEYS, ts), f32),
            pltpu.VMEM((2, N_KEYS, ts), f32),
            pltpu.VMEM((eb, ts), f32),
            pltpu.VMEM((eb, ts), f32),
            pltpu.VMEM((eb, ts), bf16),
            pltpu.VMEM((eb, ts), bf16),
            pltpu.VMEM((d, ts), f32),
        ],
        compiler_params=pltpu.CompilerParams(
            dimension_semantics=("parallel", "arbitrary"),
            vmem_limit_bytes=V7X_VMEM_LIMIT_BYTES),
        name="peer",
    )(x2, mod, n2g, wpq_t, keys, down, down, down, up_t)
    return out.reshape(batch, seq, d)


def _t5_causal_bucket(dist):
    max_exact = N_BUCKETS // 2
    d = jnp.maximum(dist, 0)
    log_ratio = jnp.log(jnp.maximum(d, 1).astype(f32) / max_exact) / math.log(MAX_DISTANCE / max_exact)
    large = max_exact + (log_ratio * (N_BUCKETS - max_exact)).astype(jnp.int32)
    large = jnp.minimum(large, N_BUCKETS - 1)
    return jnp.where(d < max_exact, d, large)


def _bias_table(rel_bias):
    q_idx = jnp.arange(ATT_BLOCK)[:, None]
    s_idx = jnp.arange(2 * ATT_BLOCK)[None, :]
    dist = q_idx + ATT_BLOCK - s_idx
    in_window = (dist >= 0) & (dist < WINDOW)
    bias = rel_bias.astype(f32)[_t5_causal_bucket(dist)].transpose(2, 0, 1)
    return jnp.where(in_window[None], bias, NEG_INF)


def _forward(x, c, rel_bias, w_ada, b_ada, norm1_g, norm2_g, w_in, q_norm_g, k_norm_g, attn_sink,
             conv_w, sgu_w, sgu_b, out_norm_g, w_out, peer_wq, peer_sub_keys, peer_down, peer_up,
             *, mixer_ts, peer_ts, peer_eb):
    depth = w_ada.shape[0]
    d = x.shape[-1]
    mod = _adaln(c, w_ada, b_ada)
    bias_tbl = _bias_table(rel_bias)
    grp = jnp.arange(D_ATTN) // HEAD_DIM
    bd = (grp[:, None] == grp[None, :]).astype(bf16)
    for l in range(depth):
        sgu_bfull = jnp.repeat(sgu_b[l].T, SGU_GROUP_DIM, axis=1)
        x = _mixer(
            x, mod[l], norm1_g[l].reshape(1, d), w_in[l].astype(bf16),
            jnp.tile(q_norm_g[l], N_Q_HEADS).reshape(1, D_ATTN),
            jnp.tile(k_norm_g[l], N_KV_HEADS).reshape(1, D_KV),
            attn_sink[l], bias_tbl, conv_w[l], sgu_w[l], sgu_bfull,
            out_norm_g[l].reshape(1, d), w_out[l].astype(bf16), bd, ts=mixer_ts)
        x = _peer(
            x, mod[l], norm2_g[l].reshape(1, d), peer_wq[l].T.astype(bf16),
            peer_sub_keys[l].reshape(2 * PEER_HEADS, N_KEYS, PEER_HALF).astype(bf16),
            (peer_down[l] * SQRT_HALF).astype(bf16), peer_up[l].T.astype(bf16), ts=peer_ts, eb=peer_eb)
    return x


def kernel(x, c, rel_bias, w_ada, b_ada, norm1_g, norm2_g, w_in, q_norm_g, k_norm_g, attn_sink, conv_w,
           sgu_w, sgu_b, out_norm_g, w_out, peer_wq, peer_sub_keys, peer_down, peer_up):
    return _forward(x, c, rel_bias, w_ada, b_ada, norm1_g, norm2_g, w_in, q_norm_g, k_norm_g, attn_sink,
                    conv_w, sgu_w, sgu_b, out_norm_g, w_out, peer_wq, peer_sub_keys, peer_down, peer_up,
                    mixer_ts=MIXER_TOKENS, peer_ts=PEER_TOKENS, peer_eb=PEER_EXPERT_BLOCK)
```
